```python
import numpy as np
import jax
import jax.numpy as jnp
from jax import lax

D_MODEL = 2048
BATCH = 1
SEQ = 16384
DEPTH = 1
DEC_BATCH = 32
DEC_SEQ = 1
PAST_LEN = 16384
PAGE_SIZE = 128

HEAD_DIM = 128
SB_HEADS = 8
NSA_Q_HEADS = 8
NSA_KV_HEADS = 2
NSA_GROUP = NSA_Q_HEADS // NSA_KV_HEADS
CMP_BLOCK = 32
CMP_STRIDE = 16
SEL_BLOCK = 64
SEL_TOPN = 16
WINDOW = 512
D_FF = 5504
ROPE_THETA = 10000.0
NORM_EPS = 1e-6
Q_BLOCK = 128
HALF_STEP = 0.5
SCALE = HEAD_DIM ** -0.5
NEG_INF = -1e30
FORCED_SCORE = 1e4
TINY = 1e-30

SB_Q_COLS = SB_HEADS * HEAD_DIM
SB_KV_COLS = 2 * SB_HEADS * HEAD_DIM
NSA_Q_COLS = NSA_Q_HEADS * HEAD_DIM
NSA_KV_COLS = 6 * NSA_KV_HEADS * HEAD_DIM
NSA_GATE_COLS = 3 * NSA_Q_HEADS
MERGE_GATE_COLS = 2 * D_MODEL
OFF_SB_KV = SB_Q_COLS
OFF_NSA_Q = OFF_SB_KV + SB_KV_COLS
OFF_NSA_KV = OFF_NSA_Q + NSA_Q_COLS
OFF_NSA_G = OFF_NSA_KV + NSA_KV_COLS
OFF_MERGE_G = OFF_NSA_G + NSA_GATE_COLS
IN_COLS = OFF_MERGE_G + MERGE_GATE_COLS

kernel_name = "hybrid_stickbreak_nsa_macaron_step"


def rms_norm(x, g):
    x32 = x.astype(jnp.float32)
    y = x32 * lax.rsqrt(jnp.mean(x32 * x32, axis=-1, keepdims=True) + NORM_EPS)
    return (y * g.astype(jnp.float32)).astype(x.dtype)


def swiglu(x, w_gu, w_down):
    gate, up = jnp.split(x @ w_gu, 2, axis=-1)
    return (jax.nn.silu(gate) * up) @ w_down


def rope(x, pos):
    half = HEAD_DIM // 2
    inv_freq = ROPE_THETA ** (-2.0 * jnp.arange(half, dtype=jnp.float32) / HEAD_DIM)
    ang = pos.astype(jnp.float32)[:, None] * inv_freq[None, :]
    shape = (pos.shape[0],) + (1,) * (x.ndim - 3) + (half,)
    cos = jnp.cos(ang).reshape(shape)
    sin = jnp.sin(ang).reshape(shape)
    x32 = x.astype(jnp.float32)
    x1, x2 = x32[..., :half], x32[..., half:]
    return jnp.concatenate([x1 * cos - x2 * sin, x2 * cos + x1 * sin], axis=-1).astype(x.dtype)


def masked_softmax(s, mask):
    s = jnp.where(mask, s, NEG_INF)
    m = jnp.max(s, axis=-1, keepdims=True)
    p = jnp.where(mask, jnp.exp(s - m), 0.0)
    return p / jnp.maximum(jnp.sum(p, axis=-1, keepdims=True), TINY)


def stick_breaking_weights(z, mask):
    log_beta = jax.nn.log_sigmoid(z)
    log_keep = jnp.where(mask, jax.nn.log_sigmoid(-z), 0.0)
    later = lax.cumsum(log_keep, axis=z.ndim - 1, reverse=True) - log_keep
    return jnp.where(mask, jnp.exp(log_beta + later), 0.0)


def split_mixer_inputs(h, w_in, pos):
    B, T, _ = h.shape
    p = h @ w_in
    sb_q = p[..., :OFF_SB_KV].reshape(B, T, SB_HEADS, HEAD_DIM)
    sb_kv = p[..., OFF_SB_KV:OFF_NSA_Q].reshape(B, T, 2, SB_HEADS, HEAD_DIM)
    nsa_q = p[..., OFF_NSA_Q:OFF_NSA_KV].reshape(B, T, NSA_KV_HEADS, NSA_GROUP, HEAD_DIM)
    nsa_kv = p[..., OFF_NSA_KV:OFF_NSA_G].reshape(B, T, 6, NSA_KV_HEADS, HEAD_DIM)
    nsa_rows = jnp.stack([nsa_kv[:, :, 0], nsa_kv[:, :, 1], rope(nsa_kv[:, :, 2], pos), nsa_kv[:, :, 3]], axis=2)
    win_rows = jnp.stack([rope(nsa_kv[:, :, 4], pos), nsa_kv[:, :, 5]], axis=2)
    nsa_gates = jax.nn.sigmoid(p[..., OFF_NSA_G:OFF_MERGE_G].astype(jnp.float32)).reshape(B, T, 3, NSA_KV_HEADS, NSA_GROUP)
    merge_gates = jax.nn.sigmoid(p[..., OFF_MERGE_G:].astype(jnp.float32)).reshape(B, T, 2, D_MODEL)
    return sb_q, sb_kv, nsa_q, rope(nsa_q, pos), nsa_rows, win_rows, nsa_gates, merge_gates


def merge_branches(o_sb, o_nsa, merge_gates, p_a, p_b, w_o, dtype):
    B, T = o_sb.shape[:2]
    a = o_sb.reshape(B, T, -1).astype(dtype) @ p_a
    b = o_nsa.reshape(B, T, -1).astype(dtype) @ p_b
    m = merge_gates[:, :, 0] * a.astype(jnp.float32) + merge_gates[:, :, 1] * b.astype(jnp.float32)
    return m.astype(dtype) @ w_o


def compress(k, pe, w):
    B, T, G, d = k.shape
    n_cmp = (T - CMP_BLOCK) // CMP_STRIDE + 1
    n_sub = CMP_BLOCK // CMP_STRIDE
    chunks = k[:, :(n_cmp + n_sub - 1) * CMP_STRIDE].reshape(B, n_cmp + n_sub - 1, CMP_STRIDE, G, d)
    pe_r = pe.reshape(n_sub, CMP_STRIDE, d)
    w_r = w.reshape(n_sub, CMP_STRIDE, d, d)
    out = jnp.einsum("bcjgd,jde->bcge", chunks[:, :n_cmp] + pe_r[0][:, None, :], w_r[0])
    for m in range(1, n_sub):
        out = out + jnp.einsum("bcjgd,jde->bcge", chunks[:, m:m + n_cmp] + pe_r[m][:, None, :], w_r[m])
    ends = np.arange(n_cmp) * CMP_STRIDE + CMP_BLOCK - 1
    return out, ends


def to_sel_blocks(k):
    B, T, G, d = k.shape
    n_sel = -(-T // SEL_BLOCK)
    k = jnp.pad(k, ((0, 0), (0, n_sel * SEL_BLOCK - T), (0, 0), (0, 0)))
    return k.reshape(B, n_sel, SEL_BLOCK, G, d).transpose(0, 3, 1, 2, 4)


def selection_map(n_cmp, n_sel):
    ratio = SEL_BLOCK // CMP_STRIDE
    r = np.arange(-((CMP_BLOCK - 1) // CMP_STRIDE), (SEL_BLOCK - 1) // CMP_STRIDE + 1)
    idx = ratio * np.arange(n_sel)[:, None] + r[None, :]
    return np.where((idx >= 0) & (idx < n_cmp), idx, n_cmp).astype(np.int32)


def nsa_attend(q_raw, q_rot, q_pos, gates, ck, cv, cmp_end, sk, sv, wk, wv, w_pos):
    B, Q, G, R, d = q_rot.shape
    n_cmp, n_sel = ck.shape[1], sk.shape[2]
    s = jnp.einsum("bqgrd,bcgd->bqgrc", q_raw, ck).astype(jnp.float32) * SCALE
    cmask = (jnp.asarray(cmp_end)[None, :] <= q_pos[:, None])[None, :, None, None, :]
    p_cmp = masked_softmax(s, cmask)
    o_cmp = jnp.einsum("bqgrc,bcgd->bqgrd", p_cmp, cv.astype(jnp.float32))
    imp = jnp.sum(p_cmp, axis=3)
    imp = jnp.concatenate([imp, jnp.zeros_like(imp[..., :1])], axis=-1)
    imp_sel = jnp.sum(imp[..., selection_map(n_cmp, n_sel)], axis=-1)
    blk = jnp.arange(n_sel)[None, :]
    cur = (q_pos // SEL_BLOCK)[:, None]
    eligible = (blk * SEL_BLOCK <= q_pos[:, None])[None, :, None, :]
    forced = ((blk == 0) | (blk == cur) | (blk == cur - 1))[None, :, None, :]
    score = jnp.where(eligible, jnp.where(forced, FORCED_SCORE, imp_sel), NEG_INF)
    n_top = min(SEL_TOPN, n_sel)
    top_s, top_i = lax.top_k(score, n_top)
    valid = top_s > 0.5 * NEG_INF
    gather = jax.vmap(jax.vmap(lambda blocks, ids: blocks[ids]))
    idx = top_i.transpose(0, 2, 1, 3)
    k_sel = gather(sk, idx)
    v_sel = gather(sv, idx).reshape(B, G, Q, n_top * SEL_BLOCK, d)
    k_pos = top_i[..., None] * SEL_BLOCK + jnp.arange(SEL_BLOCK)
    smask = (valid[..., None] & (k_pos <= q_pos[None, :, None, None, None])).reshape(B, Q, G, 1, n_top * SEL_BLOCK)
    s = jnp.einsum("bqgrd,bgqnld->bqgrnl", q_rot, k_sel).astype(jnp.float32).reshape(B, Q, G, R, n_top * SEL_BLOCK) * SCALE
    o_sel = jnp.einsum("bqgrm,bgqmd->bqgrd", masked_softmax(s, smask), v_sel.astype(jnp.float32))
    s = jnp.einsum("bqgrd,bwgd->bqgrw", q_rot, wk).astype(jnp.float32) * SCALE
    diff = q_pos[:, None] - w_pos[None, :]
    wmask = ((diff >= 0) & (diff <= WINDOW) & (w_pos[None, :] >= 0))[None, :, None, None, :]
    o_win = jnp.einsum("bqgrw,bwgd->bqgrd", masked_softmax(s, wmask), wv.astype(jnp.float32))
    o = gates[:, :, 0][..., None] * o_cmp + gates[:, :, 1][..., None] * o_sel + gates[:, :, 2][..., None] * o_win
    return o.reshape(B, Q, G * R, d)


def sb_prompt(q, kv):
    B, T = q.shape[:2]
    k = kv[:, :, 0]
    v32 = kv[:, :, 1].astype(jnp.float32)
    k_pos = jnp.arange(T)

    def block(i):
        q0 = i * Q_BLOCK
        qb = lax.dynamic_slice_in_dim(q, q0, Q_BLOCK, axis=1)
        z = jnp.einsum("bqhd,bkhd->bhqk", qb, k).astype(jnp.float32) * SCALE
        q_pos = q0 + jnp.arange(Q_BLOCK)
        a = stick_breaking_weights(z, k_pos[None, :] < q_pos[:, None])
        return jnp.einsum("bhqk,bkhd->bqhd", a, v32)

    out = lax.map(block, jnp.arange(T // Q_BLOCK))
    return out.transpose(1, 0, 2, 3, 4).reshape(B, T, SB_HEADS, HEAD_DIM)


def sb_sample(q, kv_new, cache_sb_kv, page_table, layer):
    DB, DS = q.shape[:2]
    past = page_table.shape[1] * PAGE_SIZE
    k_past = cache_sb_kv[layer, page_table, :, 0].reshape(DB, past, SB_HEADS, HEAD_DIM)
    v_past = cache_sb_kv[layer, page_table, :, 1].reshape(DB, past, SB_HEADS, HEAD_DIM)
    z = jnp.concatenate([jnp.einsum("bqhd,bkhd->bhqk", q, k_past),
                         jnp.einsum("bqhd,bkhd->bhqk", q, kv_new[:, :, 0])], axis=-1).astype(jnp.float32) * SCALE
    q_pos = past + jnp.arange(DS)
    k_pos = jnp.arange(past + DS)
    a = stick_breaking_weights(z, k_pos[None, :] < q_pos[:, None])
    return (jnp.einsum("bhqk,bkhd->bqhd", a[..., :past], v_past.astype(jnp.float32))
            + jnp.einsum("bhqk,bkhd->bqhd", a[..., past:], kv_new[:, :, 1].astype(jnp.float32)))


def nsa_prompt(q_raw, q_rot, gates, nsa_rows, win_rows, pe_k, w_k, pe_v, w_v):
    B, T = q_raw.shape[:2]
    ck, ends = compress(nsa_rows[:, :, 0], pe_k, w_k)
    cv, _ = compress(nsa_rows[:, :, 1], pe_v, w_v)
    sk = to_sel_blocks(nsa_rows[:, :, 2])
    sv = to_sel_blocks(nsa_rows[:, :, 3])
    pad = ((0, 0), (WINDOW, 0), (0, 0), (0, 0))
    wk = jnp.pad(win_rows[:, :, 0], pad)
    wv = jnp.pad(win_rows[:, :, 1], pad)

    def block(i):
        q0 = i * Q_BLOCK
        sl = lambda a: lax.dynamic_slice_in_dim(a, q0, Q_BLOCK, axis=1)
        wsl = lambda a: lax.dynamic_slice_in_dim(a, q0, WINDOW + Q_BLOCK, axis=1)
        q_pos = q0 + jnp.arange(Q_BLOCK)
        w_pos = q0 - WINDOW + jnp.arange(WINDOW + Q_BLOCK)
        return nsa_attend(sl(q_raw), sl(q_rot), q_pos, sl(gates), ck, cv, ends, sk, sv, wsl(wk), wsl(wv), w_pos)

    out = lax.map(block, jnp.arange(T // Q_BLOCK))
    return out.transpose(1, 0, 2, 3, 4).reshape(B, T, NSA_Q_HEADS, HEAD_DIM)


def nsa_sample(q_raw, q_rot, gates, nsa_rows, win_rows, cache_nsa_kv, win_state, page_table, layer, pe_k, w_k, pe_v, w_v):
    DB, DS = q_raw.shape[:2]
    past = page_table.shape[1] * PAGE_SIZE
    win_buf = win_state.shape[1]

    def full_rows(c):
        rows = cache_nsa_kv[layer, page_table, :, c].reshape(DB, past, NSA_KV_HEADS, HEAD_DIM)
        return jnp.concatenate([rows, nsa_rows[:, :, c]], axis=1)

    ck, ends = compress(full_rows(0), pe_k, w_k)
    cv, _ = compress(full_rows(1), pe_v, w_v)
    sk = to_sel_blocks(full_rows(2))
    sv = to_sel_blocks(full_rows(3))
    win_all = jnp.concatenate([win_state, win_rows], axis=1)
    w_pos = past - win_buf + jnp.arange(win_buf + DS)
    q_pos = past + jnp.arange(DS)
    o = nsa_attend(q_raw, q_rot, q_pos, gates, ck, cv, ends, sk, sv, win_all[:, :, 0], win_all[:, :, 1], w_pos)
    return o, win_all[:, win_all.shape[1] - win_buf:]


def macaron_layer(x, pos, token_mixer, g_ffn1, w1_gu, w1_down, g_mix, w_in, p_a, p_b, w_o, g_ffn2, w2_gu, w2_down):
    x = x + HALF_STEP * swiglu(rms_norm(x, g_ffn1), w1_gu, w1_down)
    sb_q, sb_kv, nq, nq_rot, nsa_rows, win_rows, nsa_gates, merge_gates = split_mixer_inputs(rms_norm(x, g_mix), w_in, pos)
    o_sb, o_nsa, new_win = token_mixer(sb_q, sb_kv, nq, nq_rot, nsa_rows, win_rows, nsa_gates)
    x = x + merge_branches(o_sb, o_nsa, merge_gates, p_a, p_b, w_o, x.dtype)
    x = x + HALF_STEP * swiglu(rms_norm(x, g_ffn2), w2_gu, w2_down)
    return x, sb_kv, nsa_rows, new_win


def setup_inputs(seed: int = 0) -> dict:
    key = jax.random.key(seed)
    ks = jax.random.split(key, 24)
    n_pages = PAST_LEN // PAGE_SIZE
    n_used = DEC_BATCH * n_pages
    n_pool = n_used + n_used // 4
    win_buf = min(WINDOW, PAST_LEN)
    f32 = jnp.float32

    def nrm(k, shape, scale):
        return scale * jax.random.normal(k, shape, f32)

    def gain(k, shape):
        return 1.0 + 0.1 * jax.random.normal(k, shape, f32)

    page_table = jax.random.permutation(ks[5], n_pool)[:n_used].reshape(DEC_BATCH, n_pages).astype(jnp.int32)
    return {
        "x_prompt": nrm(ks[0], (BATCH, SEQ, D_MODEL), 1.0),
        "x_sample": nrm(ks[1], (DEC_BATCH, DEC_SEQ, D_MODEL), 1.0),
        "cache_sb_kv": nrm(ks[2], (DEPTH, n_pool, PAGE_SIZE, 2, SB_HEADS, HEAD_DIM), 1.0),
        "cache_nsa_kv": nrm(ks[3], (DEPTH, n_pool, PAGE_SIZE, 4, NSA_KV_HEADS, HEAD_DIM), 1.0),
        "state_win_kv": nrm(ks[4], (DEPTH, DEC_BATCH, win_buf, 2, NSA_KV_HEADS, HEAD_DIM), 1.0),
        "page_table": page_table,
        "g_ffn1": gain(ks[6], (DEPTH, D_MODEL)),
        "ffn1_w_gu": nrm(ks[7], (DEPTH, D_MODEL, 2 * D_FF), D_MODEL ** -0.5),
        "ffn1_w_down": nrm(ks[8], (DEPTH, D_FF, D_MODEL), D_FF ** -0.5),
        "g_mix": gain(ks[9], (DEPTH, D_MODEL)),
        "w_in": nrm(ks[10], (DEPTH, D_MODEL, IN_COLS), D_MODEL ** -0.5),
        "cmp_pe_k": nrm(ks[11], (DEPTH, CMP_BLOCK, HEAD_DIM), 0.1),
        "cmp_w_k": nrm(ks[12], (DEPTH, CMP_BLOCK, HEAD_DIM, HEAD_DIM), (CMP_BLOCK * HEAD_DIM) ** -0.5),
        "cmp_pe_v": nrm(ks[13], (DEPTH, CMP_BLOCK, HEAD_DIM), 0.1),
        "cmp_w_v": nrm(ks[14], (DEPTH, CMP_BLOCK, HEAD_DIM, HEAD_DIM), (CMP_BLOCK * HEAD_DIM) ** -0.5),
        "p_a": nrm(ks[15], (DEPTH, SB_HEADS * HEAD_DIM, D_MODEL), (SB_HEADS * HEAD_DIM) ** -0.5),
        "p_b": nrm(ks[16], (DEPTH, NSA_Q_HEADS * HEAD_DIM, D_MODEL), (NSA_Q_HEADS * HEAD_DIM) ** -0.5),
        "w_o": nrm(ks[17], (DEPTH, D_MODEL, D_MODEL), D_MODEL ** -0.5),
        "g_ffn2": gain(ks[18], (DEPTH, D_MODEL)),
        "ffn2_w_gu": nrm(ks[19], (DEPTH, D_MODEL, 2 * D_FF), D_MODEL ** -0.5),
        "ffn2_w_down": nrm(ks[20], (DEPTH, D_FF, D_MODEL), D_FF ** -0.5),
        "g_final": gain(ks[21], (D_MODEL,)),
    }


def reference(x_prompt, x_sample, cache_sb_kv, cache_nsa_kv, state_win_kv, page_table,
              g_ffn1, ffn1_w_gu, ffn1_w_down, g_mix, w_in, cmp_pe_k, cmp_w_k, cmp_pe_v, cmp_w_v,
              p_a, p_b, w_o, g_ffn2, ffn2_w_gu, ffn2_w_down, g_final):
    T = x_prompt.shape[1]
    DS = x_sample.shape[1]
    past = page_table.shape[1] * PAGE_SIZE
    pos_p = jnp.arange(T)
    pos_s = past + jnp.arange(DS)
    xp, xs = x_prompt, x_sample
    sb_p, nsa_p, win_p, sb_s, nsa_s, win_s = [], [], [], [], [], []
    for layer in range(DEPTH):
        ffn_and_proj = (g_ffn1[layer], ffn1_w_gu[layer], ffn1_w_down[layer], g_mix[layer], w_in[layer],
                        p_a[layer], p_b[layer], w_o[layer], g_ffn2[layer], ffn2_w_gu[layer], ffn2_w_down[layer])
        cmp_w = (cmp_pe_k[layer], cmp_w_k[layer], cmp_pe_v[layer], cmp_w_v[layer])

        def prompt_mixer(sb_q, sb_kv, nq, nq_rot, nsa_rows, win_rows, nsa_gates):
            o_sb = sb_prompt(sb_q, sb_kv)
            o_nsa = nsa_prompt(nq, nq_rot, nsa_gates, nsa_rows, win_rows, *cmp_w)
            return o_sb, o_nsa, win_rows[:, T - min(WINDOW, T):]

        def sample_mixer(sb_q, sb_kv, nq, nq_rot, nsa_rows, win_rows, nsa_gates):
            o_sb = sb_sample(sb_q, sb_kv, cache_sb_kv, page_table, layer)
            o_nsa, new_win = nsa_sample(nq, nq_rot, nsa_gates, nsa_rows, win_rows, cache_nsa_kv,
                                        state_win_kv[layer], page_table, layer, *cmp_w)
            return o_sb, o_nsa, new_win

        xp, kv_a, kv_b, kv_w = macaron_layer(xp, pos_p, prompt_mixer, *ffn_and_proj)
        sb_p.append(kv_a)
        nsa_p.append(kv_b)
        win_p.append(kv_w)
        xs, kv_a, kv_b, kv_w = macaron_layer(xs, pos_s, sample_mixer, *ffn_and_proj)
        sb_s.append(kv_a)
        nsa_s.append(kv_b)
        win_s.append(kv_w)
    y_prompt = rms_norm(xp, g_final)
    y_sample = rms_norm(xs, g_final)
    return (y_prompt, y_sample, jnp.stack(sb_p), jnp.stack(nsa_p), jnp.stack(win_p),
            jnp.stack(sb_s), jnp.stack(nsa_s), jnp.stack(win_s))
```

```python
import functools

import numpy as np
import jax
import jax.numpy as jnp
from jax import lax
from jax.experimental import pallas as pl
from jax.experimental.pallas import tpu as pltpu

HEAD_DIM = 128
SB_HEADS = 8
NSA_Q_HEADS = 8
NSA_KV_HEADS = 2
NSA_GROUP = NSA_Q_HEADS // NSA_KV_HEADS
CMP_BLOCK = 32
CMP_STRIDE = 16
SEL_BLOCK = 64
SEL_TOPN = 16
WINDOW = 512
PAGE_SIZE = 128
ROPE_THETA = 10000.0
NORM_EPS = 1e-6
HALF_STEP = 0.5
SCALE = HEAD_DIM ** -0.5
NEG_INF = -1e30
FORCED_SCORE = 1e4
TINY = 1e-30

SB_DEAD_LOG = -104.0

LANES = 128
VMEM_LIMIT = 56 * 1024 * 1024

F32 = jnp.float32
BF16 = jnp.bfloat16


def _cparams(sem):
    return pltpu.CompilerParams(dimension_semantics=sem, vmem_limit_bytes=VMEM_LIMIT)


def _round_up(x, m):
    return -(-x // m) * m


def _pick_tile(n, pref):
    t = min(pref, n)
    while n % t:
        t //= 2
    return t


def _rms(x, g):
    return x * lax.rsqrt(jnp.mean(x * x, axis=-1, keepdims=True) + NORM_EPS) * g


def _sigmoid(x):
    return 1.0 / (1.0 + jnp.exp(-x))


def _ffn_kernel(x_ref, g_ref, wg_ref, wu_ref, wd_ref, g2_ref, *refs, emit_x):
    if emit_x:
        out_ref, n_ref, h_scr, acc_scr = refs
    else:
        n_ref, h_scr, acc_scr = refs
    j = pl.program_id(1)

    @pl.when(j == 0)
    def _():
        h_scr[...] = _rms(x_ref[...], g_ref[...]).astype(BF16)
        acc_scr[...] = jnp.zeros_like(acc_scr)

    h = h_scr[...]
    gate = jnp.dot(h, wg_ref[...], preferred_element_type=F32)
    up = jnp.dot(h, wu_ref[...], preferred_element_type=F32)
    act = (gate * _sigmoid(gate) * up).astype(BF16)
    acc_scr[...] += jnp.dot(act, wd_ref[...], preferred_element_type=F32)

    @pl.when(j == pl.num_programs(1) - 1)
    def _():
        y = x_ref[...] + HALF_STEP * acc_scr[...]
        if emit_x:
            out_ref[...] = y
        n_ref[...] = _rms(y, g2_ref[...]).astype(n_ref.dtype)


def _ffn(x, g, wg, wu, wd, g2, *, emit_x, norm_dtype, tm_pref=512, tf=512):
    m, d = x.shape
    fpad = wg.shape[1]
    tm = _pick_tile(m, tm_pref)
    grid = (m // tm, fpad // tf)
    out_shape = []
    out_specs = []
    if emit_x:
        out_shape.append(jax.ShapeDtypeStruct((m, d), F32))
        out_specs.append(pl.BlockSpec((tm, d), lambda i, j: (i, 0)))
    out_shape.append(jax.ShapeDtypeStruct((m, d), norm_dtype))
    out_specs.append(pl.BlockSpec((tm, d), lambda i, j: (i, 0)))
    return pl.pallas_call(
        functools.partial(_ffn_kernel, emit_x=emit_x),
        grid=grid,
        in_specs=[
            pl.BlockSpec((tm, d), lambda i, j: (i, 0)),
            pl.BlockSpec((1, d), lambda i, j: (0, 0)),
            pl.BlockSpec((d, tf), lambda i, j: (0, j)),
            pl.BlockSpec((d, tf), lambda i, j: (0, j)),
            pl.BlockSpec((tf, d), lambda i, j: (j, 0)),
            pl.BlockSpec((1, d), lambda i, j: (0, 0)),
        ],
        out_specs=out_specs,
        out_shape=out_shape,
        scratch_shapes=[pltpu.VMEM((tm, d), BF16), pltpu.VMEM((tm, d), F32)],
        compiler_params=_cparams(("parallel", "arbitrary")),
        name="ffn",
    )(x, g, wg, wu, wd, g2)


def _prep_ffn_weights(w_gu, w_down, tf=512):
    d, two_f = w_gu.shape
    f = two_f // 2
    fpad = _round_up(f, tf)
    wg = jnp.pad(w_gu[:, :f].astype(BF16), ((0, 0), (0, fpad - f)))
    wu = jnp.pad(w_gu[:, f:].astype(BF16), ((0, 0), (0, fpad - f)))
    wd = jnp.pad(w_down.astype(BF16), ((0, fpad - f), (0, 0)))
    return wg, wu, wd


def _rope_tile(x, cos2, sin2):
    return x * cos2 + pltpu.roll(x, HEAD_DIM // 2, axis=1) * sin2


def _proj_kernel(a_ref, w_ref, *refs, outs, use_rope):
    if use_rope:
        cos_ref, sin_ref = refs[:2]
        refs = refs[2:]
    j = pl.program_id(1)
    acc = jnp.dot(a_ref[...], w_ref[...], preferred_element_type=F32)
    tn = acc.shape[1]
    for (mode, rope_tiles, _), o_ref in zip(outs, refs):
        if mode == "sigmoid":
            o_ref[...] = _sigmoid(acc).astype(o_ref.dtype)
        elif mode == "rope":
            cos2 = cos_ref[...]
            sin2 = sin_ref[...]
            rot = jnp.concatenate(
                [_rope_tile(acc[:, c * LANES:(c + 1) * LANES], cos2, sin2) for c in range(tn // LANES)], axis=1)
            if rope_tiles is None:
                o_ref[...] = rot.astype(o_ref.dtype)
            else:
                is_rope = functools.reduce(jnp.logical_or, [j == t for t in rope_tiles])
                o_ref[...] = jnp.where(is_rope, rot, acc).astype(o_ref.dtype)
        else:
            o_ref[...] = acc.astype(o_ref.dtype)


def _proj(a, w, outs, cos2=None, sin2=None, *, tm_pref=1024, tn=256):
    m, k = a.shape
    n = w.shape[1]
    tm = _pick_tile(m, tm_pref)
    tn = min(tn, n)
    use_rope = any(mode == "rope" for mode, _, _ in outs)
    in_specs = [pl.BlockSpec((tm, k), lambda i, j: (i, 0)), pl.BlockSpec((k, tn), lambda i, j: (0, j))]
    args = [a, w]
    if use_rope:
        in_specs += [pl.BlockSpec((tm, LANES), lambda i, j: (i, 0))] * 2
        args += [cos2, sin2]
    res = pl.pallas_call(
        functools.partial(_proj_kernel, outs=outs, use_rope=use_rope),
        grid=(m // tm, n // tn),
        in_specs=in_specs,
        out_specs=[pl.BlockSpec((tm, tn), lambda i, j: (i, j)) for _ in outs],
        out_shape=[jax.ShapeDtypeStruct((m, n), dt) for _, _, dt in outs],
        compiler_params=_cparams(("parallel", "arbitrary")),
        name="proj",
    )(*args)
    return res


def _rope_tables(pos):
    half = HEAD_DIM // 2
    inv_freq = ROPE_THETA ** (-2.0 * jnp.arange(half, dtype=F32) / HEAD_DIM)
    ang = pos.astype(F32)[:, None] * inv_freq[None, :]
    cos, sin = jnp.cos(ang), jnp.sin(ang)
    return jnp.concatenate([cos, cos], axis=1), jnp.concatenate([-sin, sin], axis=1)


def _later_matrix(tk):
    j = np.arange(tk)[:, None]
    s = np.arange(tk)[None, :]
    return jnp.asarray((j > s).astype(np.float32), dtype=BF16)


def _sb_tile(z, v, u, carry, mask):
    soft = jnp.log(1.0 + jnp.exp(-jnp.abs(z)))
    log_beta = jnp.minimum(z, 0.0) - soft
    log_keep = jnp.where(mask, jnp.minimum(-z, 0.0) - soft, 0.0)
    hi = log_keep.astype(BF16)
    lo = (log_keep - hi.astype(F32)).astype(BF16)
    later = (jnp.dot(hi, u, preferred_element_type=F32) + jnp.dot(lo, u, preferred_element_type=F32)) + carry
    a = jnp.where(mask, jnp.exp(log_beta + later), 0.0)
    pv = jnp.dot(a.astype(BF16), v, preferred_element_type=F32)
    return pv, carry + jnp.sum(log_keep, axis=1, keepdims=True)


def _sb_prompt_kernel(q_ref, k_ref, v_ref, u_ref, o_ref, acc_ref, carry_ref, *, tq, tk):
    qi = pl.program_id(1)
    acc_ref[...] = jnp.zeros_like(acc_ref)
    carry_ref[...] = jnp.zeros_like(carry_ref)
    q = q_ref[...]
    u = u_ref[...]
    qpos = qi * tq + lax.broadcasted_iota(jnp.int32, (tq, tk), 0)
    lane = lax.broadcasted_iota(jnp.int32, (tq, tk), 1)

    def cond(state):
        j, done = state
        return jnp.logical_and(j >= 0, done == 0)

    def body(state):
        j, _ = state
        start = pl.multiple_of(j * tk, tk)
        k = k_ref[pl.ds(start, tk), :]
        v = v_ref[pl.ds(start, tk), :]
        z = lax.dot_general(q, k, (((1,), (1,)), ((), ())), preferred_element_type=F32) * SCALE
        mask = (start + lane) < qpos
        pv, carry = _sb_tile(z, v, u, carry_ref[...], mask)
        acc_ref[...] += pv
        carry_ref[...] = carry
        done = (jnp.max(carry) < SB_DEAD_LOG).astype(jnp.int32)
        return j - 1, done

    n_tiles = (qi + 1) * (tq // tk)
    lax.while_loop(cond, body, (n_tiles - 1, jnp.int32(0)))
    o_ref[...] = acc_ref[...].astype(o_ref.dtype)


def _sb_prompt(q_bf, kv_bf, *, tq=256, tk=256):
    t = q_bf.shape[0]
    tq = _pick_tile(t, tq)
    tk = min(tk, tq)
    return pl.pallas_call(
        functools.partial(_sb_prompt_kernel, tq=tq, tk=tk),
        grid=(SB_HEADS, t // tq),
        in_specs=[
            pl.BlockSpec((tq, HEAD_DIM), lambda h, i: (i, h)),
            pl.BlockSpec((t, HEAD_DIM), lambda h, i: (0, h)),
            pl.BlockSpec((t, HEAD_DIM), lambda h, i: (0, SB_HEADS + h)),
            pl.BlockSpec((tk, tk), lambda h, i: (0, 0)),
        ],
        out_specs=pl.BlockSpec((tq, HEAD_DIM), lambda h, i: (i, h)),
        out_shape=jax.ShapeDtypeStruct((t, SB_HEADS * HEAD_DIM), BF16),
        scratch_shapes=[pltpu.VMEM((tq, HEAD_DIM), F32), pltpu.VMEM((tq, 1), F32)],
        compiler_params=_cparams(("parallel", "arbitrary")),
        name="sb_prompt",
    )(q_bf, kv_bf, kv_bf, _later_matrix(tk))


def _compress_kernel(*refs, n_in, rows, prefetch):
    if prefetch:
        refs = refs[1:]
    in_refs = refs[:n_in]
    pe_ref, w_ref, p0_ref, p1_ref, col_ref = refs[n_in:]
    nch = rows // CMP_STRIDE
    for sg in range(2 * NSA_KV_HEADS):
        slot = sg // NSA_KV_HEADS
        for i, r in enumerate(in_refs):
            col_ref[i * rows:(i + 1) * rows, :] = r[:, sg * LANES:(sg + 1) * LANES]
        acc0 = jnp.zeros((n_in * nch, HEAD_DIM), F32)
        acc1 = jnp.zeros((n_in * nch, HEAD_DIM), F32)
        for j in range(CMP_STRIDE):
            x = col_ref[pl.ds(j, n_in * nch, stride=CMP_STRIDE), :]
            x0 = (x + pe_ref[slot, pl.ds(j, 1), :]).astype(BF16)
            x1 = (x + pe_ref[slot, pl.ds(CMP_STRIDE + j, 1), :]).astype(BF16)
            acc0 += jnp.dot(x0, w_ref[slot, j], preferred_element_type=F32)
            acc1 += jnp.dot(x1, w_ref[slot, CMP_STRIDE + j], preferred_element_type=F32)
        p0_ref[sg] = acc0
        p1_ref[sg] = acc1


def _compress_prompt(rows_f32, pe, w, *, rows_pref=2048):
    t = rows_f32.shape[0]
    rows = _pick_tile(t, rows_pref)
    nch = rows // CMP_STRIDE
    n_sg = 2 * NSA_KV_HEADS
    out_sds = jax.ShapeDtypeStruct((1, n_sg, t // CMP_STRIDE, HEAD_DIM), F32)
    out_spec = pl.BlockSpec((None, n_sg, nch, HEAD_DIM), lambda i: (0, 0, i, 0))
    return pl.pallas_call(
        functools.partial(_compress_kernel, n_in=1, rows=rows, prefetch=False),
        grid=(t // rows,),
        in_specs=[
            pl.BlockSpec((rows, n_sg * LANES), lambda i: (i, 0)),
            pl.BlockSpec(pe.shape, lambda i: (0, 0, 0)),
            pl.BlockSpec(w.shape, lambda i: (0, 0, 0, 0)),
        ],
        out_specs=[out_spec, out_spec],
        out_shape=[out_sds, out_sds],
        scratch_shapes=[pltpu.VMEM((rows, HEAD_DIM), F32)],
        compiler_params=_cparams(("parallel",)),
        name="compress_prompt",
    )(rows_f32, pe, w)


def _compress_sample(cache, page_table, pe, w, *, pages_per_step=16):
    b, n_pages = page_table.shape
    npp = _pick_tile(n_pages, pages_per_step)
    nch = PAGE_SIZE // CMP_STRIDE
    n_sg = 2 * NSA_KV_HEADS
    out_sds = jax.ShapeDtypeStruct((b, n_sg, n_pages * nch, HEAD_DIM), F32)
    out_spec = pl.BlockSpec((None, n_sg, npp * nch, HEAD_DIM), lambda bi, i, pt: (bi, 0, i, 0))

    def page_spec(k):
        return pl.BlockSpec((None, PAGE_SIZE, n_sg * LANES), lambda bi, i, pt: (pt[bi, i * npp + k], 0, 0))

    grid_spec = pltpu.PrefetchScalarGridSpec(
        num_scalar_prefetch=1,
        grid=(b, n_pages // npp),
        in_specs=[page_spec(k) for k in range(npp)] + [
            pl.BlockSpec(pe.shape, lambda bi, i, pt: (0, 0, 0)),
            pl.BlockSpec(w.shape, lambda bi, i, pt: (0, 0, 0, 0)),
        ],
        out_specs=[out_spec, out_spec],
        scratch_shapes=[pltpu.VMEM((npp * PAGE_SIZE, HEAD_DIM), F32)],
    )
    return pl.pallas_call(
        functools.partial(_compress_kernel, n_in=npp, rows=PAGE_SIZE, prefetch=True),
        grid_spec=grid_spec,
        out_shape=[out_sds, out_sds],
        compiler_params=_cparams(("parallel", "arbitrary")),
        name="compress_sample",
    )(page_table, *([cache] * npp), pe, w)


def _selection_matrix(n_c, n_cmp, n_sel, n_sel_pad):
    ratio = SEL_BLOCK // CMP_STRIDE
    lo = -((CMP_BLOCK - 1) // CMP_STRIDE)
    hi = (SEL_BLOCK - 1) // CMP_STRIDE
    c = np.arange(n_c)[:, None]
    b = np.arange(n_sel_pad)[None, :]
    hit = (c >= ratio * b + lo) & (c <= ratio * b + hi) & (c < n_cmp) & (b < n_sel)
    return jnp.asarray(hit.astype(np.float32), dtype=BF16)


def _split_dot(x, m01):
    hi = x.astype(BF16)
    lo = (x - hi.astype(F32)).astype(BF16)
    return jnp.dot(hi, m01, preferred_element_type=F32) + jnp.dot(lo, m01, preferred_element_type=F32)


def _nsa_cmp_kernel(q_ref, p0k_ref, p1k_ref, p0v_ref, p1v_ref, smap_ref, ocmp_ref, sel_ref, idx_ref,
                    *, tq, qpos0, n_sel, n_top):
    qi = pl.program_id(2)
    n_c = p0k_ref.shape[0]
    n_sel_pad = smap_ref.shape[1]
    ck = (p0k_ref[...] + pltpu.roll(p1k_ref[...], n_c - 1, axis=0)).astype(BF16)
    cv = (p0v_ref[...] + pltpu.roll(p1v_ref[...], n_c - 1, axis=0)).astype(BF16)
    q = q_ref[...]
    qs = jnp.concatenate([q[:, r * LANES:(r + 1) * LANES] for r in range(NSA_GROUP)], axis=0)
    s = lax.dot_general(qs, ck, (((1,), (1,)), ((), ())), preferred_element_type=F32) * SCALE
    qpos_c = qpos0 + qi * tq + lax.broadcasted_iota(jnp.int32, (tq, n_c), 0)
    cend = lax.broadcasted_iota(jnp.int32, (tq, n_c), 1) * CMP_STRIDE + (CMP_BLOCK - 1)
    m1 = cend <= qpos_c
    mask = jnp.concatenate([m1] * NSA_GROUP, axis=0)
    s = jnp.where(mask, s, NEG_INF)
    p = jnp.where(mask, jnp.exp(s - jnp.max(s, axis=1, keepdims=True)), 0.0)
    p = p / jnp.maximum(jnp.sum(p, axis=1, keepdims=True), TINY)
    o = jnp.dot(p.astype(BF16), cv, preferred_element_type=F32)
    ocmp_ref[...] = jnp.concatenate([o[r * tq:(r + 1) * tq] for r in range(NSA_GROUP)], axis=1)

    imp = p[0:tq]
    for r in range(1, NSA_GROUP):
        imp = imp + p[r * tq:(r + 1) * tq]
    imp_sel = _split_dot(imp, smap_ref[...])
    qpos = qpos0 + qi * tq + lax.broadcasted_iota(jnp.int32, (tq, n_sel_pad), 0)
    blk = lax.broadcasted_iota(jnp.int32, (tq, n_sel_pad), 1)
    cur = qpos // SEL_BLOCK
    eligible = jnp.logical_and(blk * SEL_BLOCK <= qpos, blk < n_sel)
    forced = jnp.logical_or(blk == 0, jnp.logical_or(blk == cur, blk == cur - 1))
    score = jnp.where(eligible, jnp.where(forced, FORCED_SCORE, imp_sel), NEG_INF)
    blkf = blk.astype(F32)
    lane = lax.broadcasted_iota(jnp.int32, (tq, LANES), 1)
    selected = jnp.zeros((tq, n_sel_pad), F32)
    idx_acc = jnp.full((tq, LANES), -1.0, F32)
    for i in range(n_top):
        mx = jnp.max(score, axis=1, keepdims=True)
        first = jnp.min(jnp.where(score == mx, blkf, 1e9), axis=1, keepdims=True)
        valid = mx > 0.5 * NEG_INF
        hit = blkf == first
        selected = jnp.where(jnp.logical_and(hit, valid), 1.0, selected)
        idx_acc = jnp.where(lane == i, jnp.where(valid, first, -1.0), idx_acc)
        score = jnp.where(hit, -3e38, score)
    sel_ref[...] = selected
    idx_ref[...] = idx_acc.astype(jnp.int32)


def _nsa_cmp(q_bf, p0, p1, *, tq, qpos0, n_cmp, n_sel):
    b, t, _ = q_bf.shape
    n_c = p0.shape[2]
    n_sel_pad = _round_up(n_sel, LANES)
    n_top = min(SEL_TOPN, n_sel)
    smap = _selection_matrix(n_c, n_cmp, n_sel, n_sel_pad)
    gw = NSA_GROUP * HEAD_DIM

    def part_spec(slot):
        return pl.BlockSpec((None, None, n_c, HEAD_DIM), lambda bi, g, i: (bi, slot * NSA_KV_HEADS + g, 0, 0))

    return pl.pallas_call(
        functools.partial(_nsa_cmp_kernel, tq=tq, qpos0=qpos0, n_sel=n_sel, n_top=n_top),
        grid=(b, NSA_KV_HEADS, t // tq),
        in_specs=[
            pl.BlockSpec((None, tq, gw), lambda bi, g, i: (bi, i, g)),
            part_spec(0), part_spec(0), part_spec(1), part_spec(1),
            pl.BlockSpec((n_c, n_sel_pad), lambda bi, g, i: (0, 0)),
        ],
        out_specs=[
            pl.BlockSpec((None, tq, gw), lambda bi, g, i: (bi, i, g)),
            pl.BlockSpec((None, None, tq, n_sel_pad), lambda bi, g, i: (bi, g, i, 0)),
            pl.BlockSpec((None, None, tq, LANES), lambda bi, g, i: (bi, g, i, 0)),
        ],
        out_shape=[
            jax.ShapeDtypeStruct((b, t, NSA_Q_HEADS * HEAD_DIM), F32),
            jax.ShapeDtypeStruct((b, NSA_KV_HEADS, t, n_sel_pad), F32),
            jax.ShapeDtypeStruct((b, NSA_KV_HEADS, t, LANES), jnp.int32),
        ],
        compiler_params=_cparams(("parallel", "parallel", "arbitrary")),
        name="nsa_cmp",
    )(q_bf, p0, p1, p0, p1, smap)


def _stack_heads(q):
    return jnp.concatenate([q[:, r * LANES:(r + 1) * LANES] for r in range(NSA_GROUP)], axis=0)


def _masked_softmax_pv(s, mask, v):
    s = jnp.where(mask, s, NEG_INF)
    p = jnp.where(mask, jnp.exp(s - jnp.max(s, axis=1, keepdims=True)), 0.0)
    p = p / jnp.maximum(jnp.sum(p, axis=1, keepdims=True), TINY)
    return jnp.dot(p.astype(BF16), v, preferred_element_type=F32)


def _online_softmax_step(s, mask, v, m_ref, l_ref, acc_ref):
    s = jnp.where(mask, s, NEG_INF)
    m_old = m_ref[...]
    m_new = jnp.maximum(m_old, jnp.max(s, axis=1, keepdims=True))
    p = jnp.where(mask, jnp.exp(s - m_new), 0.0)
    alpha = jnp.exp(m_old - m_new)
    l_ref[...] = alpha * l_ref[...] + jnp.sum(p, axis=1, keepdims=True)
    acc_ref[...] = alpha * acc_ref[...] + jnp.dot(p.astype(BF16), v, preferred_element_type=F32)
    m_ref[...] = m_new


def _combine_branches(gates, o_cmp, o_sel, o_win, tq):
    outs = []
    for r in range(NSA_GROUP):
        rows = slice(r * tq, (r + 1) * tq)
        outs.append(gates[:, r:r + 1] * o_cmp[:, r * LANES:(r + 1) * LANES]
                    + gates[:, NSA_GROUP + r:NSA_GROUP + r + 1] * o_sel[rows]
                    + gates[:, 2 * NSA_GROUP + r:2 * NSA_GROUP + r + 1] * o_win[rows])
    return jnp.concatenate(outs, axis=1)


def _nsa_main_kernel(*refs, tq, tk, n_wblk):
    q_ref = refs[0]
    wk_refs = refs[1:1 + n_wblk]
    wv_refs = refs[1 + n_wblk:1 + 2 * n_wblk]
    sk_ref, sv_ref, sel_ref, ocmp_ref, gate_ref, o_ref, m_ref, l_ref, acc_ref = refs[1 + 2 * n_wblk:]
    qi = pl.program_id(1)
    qs = _stack_heads(q_ref[...])
    n_sel_pad = sel_ref.shape[1]

    wk = jnp.concatenate([r[...] for r in wk_refs], axis=0)
    wv = jnp.concatenate([r[...] for r in wv_refs], axis=0)
    wlen = n_wblk * tq
    qpos_w = qi * tq + lax.broadcasted_iota(jnp.int32, (tq, wlen), 0)
    wpos = (qi - (n_wblk - 1)) * tq + lax.broadcasted_iota(jnp.int32, (tq, wlen), 1)
    diff = qpos_w - wpos
    wm1 = jnp.logical_and(jnp.logical_and(diff >= 0, diff <= WINDOW), wpos >= 0)
    s_w = lax.dot_general(qs, wk, (((1,), (1,)), ((), ())), preferred_element_type=F32) * SCALE
    o_win = _masked_softmax_pv(s_w, jnp.concatenate([wm1] * NSA_GROUP, axis=0), wv)

    m_ref[...] = jnp.full_like(m_ref, NEG_INF)
    l_ref[...] = jnp.zeros_like(l_ref)
    acc_ref[...] = jnp.zeros_like(acc_ref)
    selb = sel_ref[...].astype(BF16)
    blocks_per_tile = tk // SEL_BLOCK
    rel = (lax.broadcasted_iota(jnp.int32, (n_sel_pad, tk), 0)
           - lax.broadcasted_iota(jnp.int32, (n_sel_pad, tk), 1) // SEL_BLOCK)
    qpos = qi * tq + lax.broadcasted_iota(jnp.int32, (tq, tk), 0)
    lane = lax.broadcasted_iota(jnp.int32, (tq, tk), 1)

    def body(j, carry):
        start = pl.multiple_of(j * tk, tk)
        expand = jnp.where(rel == j * blocks_per_tile, 1.0, 0.0).astype(BF16)
        picked = jnp.dot(selb, expand, preferred_element_type=F32)
        m1 = jnp.logical_and(picked > 0.5, start + lane <= qpos)
        k = sk_ref[pl.ds(start, tk), :]
        v = sv_ref[pl.ds(start, tk), :]
        s = lax.dot_general(qs, k, (((1,), (1,)), ((), ())), preferred_element_type=F32) * SCALE
        _online_softmax_step(s, jnp.concatenate([m1] * NSA_GROUP, axis=0), v, m_ref, l_ref, acc_ref)
        return carry

    n_tiles = ((qi + 1) * tq + tk - 1) // tk
    lax.fori_loop(0, n_tiles, body, 0)
    o_sel = acc_ref[...] / jnp.maximum(l_ref[...], TINY)
    o_ref[...] = _combine_branches(gate_ref[...], ocmp_ref[...], o_sel, o_win, tq).astype(o_ref.dtype)


def _nsa_main_prompt(q_rot_bf, win_bf, rows_bf, sel, o_cmp, gates, *, tq=128, tk=256):
    t = q_rot_bf.shape[0]
    tq = _pick_tile(t, tq)
    tk = _pick_tile(t, tk)
    assert WINDOW % tq == 0 and tk % SEL_BLOCK == 0
    n_wblk = WINDOW // tq + 1
    n_sel_pad = sel.shape[2]
    gw = NSA_GROUP * HEAD_DIM

    def win_spec(s, col0):
        return pl.BlockSpec((tq, HEAD_DIM), lambda g, i: (jnp.maximum(i - (n_wblk - 1) + s, 0), col0 + g))

    in_specs = [pl.BlockSpec((tq, gw), lambda g, i: (i, g))]
    in_specs += [win_spec(s, 0) for s in range(n_wblk)]
    in_specs += [win_spec(s, NSA_KV_HEADS) for s in range(n_wblk)]
    in_specs += [
        pl.BlockSpec((t, HEAD_DIM), lambda g, i: (0, 2 * NSA_KV_HEADS + g)),
        pl.BlockSpec((t, HEAD_DIM), lambda g, i: (0, 3 * NSA_KV_HEADS + g)),
        pl.BlockSpec((None, tq, n_sel_pad), lambda g, i: (g, i, 0)),
        pl.BlockSpec((tq, gw), lambda g, i: (i, g)),
        pl.BlockSpec((tq, LANES), lambda g, i: (i, g)),
    ]
    return pl.pallas_call(
        functools.partial(_nsa_main_kernel, tq=tq, tk=tk, n_wblk=n_wblk),
        grid=(NSA_KV_HEADS, t // tq),
        in_specs=in_specs,
        out_specs=pl.BlockSpec((tq, gw), lambda g, i: (i, g)),
        out_shape=jax.ShapeDtypeStruct((t, NSA_Q_HEADS * HEAD_DIM), BF16),
        scratch_shapes=[
            pltpu.VMEM((NSA_GROUP * tq, 1), F32),
            pltpu.VMEM((NSA_GROUP * tq, 1), F32),
            pltpu.VMEM((NSA_GROUP * tq, HEAD_DIM), F32),
        ],
        compiler_params=_cparams(("parallel", "arbitrary")),
        name="nsa_main_prompt",
    )(q_rot_bf, *([win_bf] * (2 * n_wblk)), rows_bf, rows_bf, sel, o_cmp, gates)


ROWS_PAD = 16


def _sb_sample_kernel(pt_ref, q_ref, new_ref, page_ref, u_ref, o_ref, qrows_ref, acc_ref, carry_ref, done_ref,
                      *, n_pages, past, n_new):
    j = pl.program_id(1)
    nk = SB_HEADS * HEAD_DIM
    own = (lax.broadcasted_iota(jnp.int32, (ROWS_PAD, nk), 1) // HEAD_DIM
           == lax.broadcasted_iota(jnp.int32, (ROWS_PAD, nk), 0))
    lane = lax.broadcasted_iota(jnp.int32, (ROWS_PAD, PAGE_SIZE), 1)
    qpos = past + n_new - 1

    def process(tile, kpos0, n_valid):
        k = tile[:, :nk].astype(BF16)
        v = tile[:, nk:].astype(BF16)
        z = lax.dot_general(qrows_ref[...], k, (((1,), (1,)), ((), ())), preferred_element_type=F32) * SCALE
        mask = jnp.logical_and(kpos0 + lane < qpos, lane < n_valid)
        pv, carry = _sb_tile(z, v, u_ref[...], carry_ref[...], mask)
        acc_ref[...] += pv
        carry_ref[...] = carry
        done_ref[0] = (jnp.max(carry[:SB_HEADS]) < SB_DEAD_LOG).astype(jnp.int32)

    @pl.when(j == 0)
    def _():
        qrows_ref[...] = jnp.where(own, jnp.broadcast_to(q_ref[...], (ROWS_PAD, nk)), 0.0).astype(BF16)
        acc_ref[...] = jnp.zeros_like(acc_ref)
        carry_ref[...] = jnp.zeros_like(carry_ref)
        row = lax.broadcasted_iota(jnp.int32, (PAGE_SIZE, 2 * nk), 0)
        new_tile = jnp.where(row < n_new, jnp.broadcast_to(new_ref[...], (PAGE_SIZE, 2 * nk)), 0.0)
        process(new_tile, past, n_new)

    @pl.when(jnp.logical_and(j > 0, done_ref[0] == 0))
    def _():
        process(page_ref[...], (n_pages - j) * PAGE_SIZE, PAGE_SIZE)

    @pl.when(j == n_pages)
    def _():
        o_ref[...] = jnp.sum(jnp.where(own, acc_ref[...], 0.0), axis=0, keepdims=True)


def _sb_sample(q, kv_new, cache, page_table):
    b, n_pages = page_table.shape
    nk = SB_HEADS * HEAD_DIM
    n_new = 1
    grid_spec = pltpu.PrefetchScalarGridSpec(
        num_scalar_prefetch=1,
        grid=(b, n_pages + 1),
        in_specs=[
            pl.BlockSpec((None, 1, nk), lambda bi, j, pt: (bi, 0, 0)),
            pl.BlockSpec((None, 1, 2 * nk), lambda bi, j, pt: (bi, 0, 0)),
            pl.BlockSpec((None, PAGE_SIZE, 2 * nk), lambda bi, j, pt: (pt[bi, n_pages - jnp.maximum(j, 1)], 0, 0)),
            pl.BlockSpec((PAGE_SIZE, PAGE_SIZE), lambda bi, j, pt: (0, 0)),
        ],
        out_specs=pl.BlockSpec((None, 1, nk), lambda bi, j, pt: (bi, 0, 0)),
        scratch_shapes=[
            pltpu.VMEM((ROWS_PAD, nk), BF16),
            pltpu.VMEM((ROWS_PAD, nk), F32),
            pltpu.VMEM((ROWS_PAD, 1), F32),
            pltpu.SMEM((1,), jnp.int32),
        ],
    )
    out = pl.pallas_call(
        functools.partial(_sb_sample_kernel, n_pages=n_pages, past=n_pages * PAGE_SIZE, n_new=n_new),
        grid_spec=grid_spec,
        out_shape=jax.ShapeDtypeStruct((b, 1, nk), F32),
        compiler_params=_cparams(("parallel", "arbitrary")),
        name="sb_sample",
    )(page_table, q[:, None, :], kv_new[:, None, :], cache, _later_matrix(PAGE_SIZE))
    return out[:, 0, :]


def _nsa_main_sample_kernel(idx_ref, pt_ref, q_ref, ck_ref, cv_ref, nk_ref, nv_ref, wk_ref, wv_ref, nwk_ref, nwv_ref,
                            ocmp_ref, gate_ref, o_ref, m_ref, l_ref, acc_ref, osel_ref, *, n_top, past, n_new):
    bi, g, n = pl.program_id(0), pl.program_id(1), pl.program_id(2)
    qpos = past + n_new - 1
    q = q_ref[...]
    win_buf = wk_ref.shape[0]

    def reset():
        m_ref[...] = jnp.full_like(m_ref, NEG_INF)
        l_ref[...] = jnp.zeros_like(l_ref)
        acc_ref[...] = jnp.zeros_like(acc_ref)

    def new_tile(ref, rows):
        row = lax.broadcasted_iota(jnp.int32, (rows, HEAD_DIM), 0)
        return jnp.where(row < n_new, jnp.broadcast_to(ref[...], (rows, HEAD_DIM)), 0.0)

    def step(k, v, mask):
        s = lax.dot_general(q, k.astype(BF16), (((1,), (1,)), ((), ())), preferred_element_type=F32) * SCALE
        _online_softmax_step(s, mask, v.astype(BF16), m_ref, l_ref, acc_ref)

    @pl.when(n == 0)
    def _():
        reset()

    blk = idx_ref[(bi * NSA_KV_HEADS + g) * n_top + n]
    is_new = blk * SEL_BLOCK >= past
    lane = lax.broadcasted_iota(jnp.int32, (ROWS_PAD, SEL_BLOCK), 1)
    k = jnp.where(is_new, new_tile(nk_ref, SEL_BLOCK), ck_ref[...])
    v = jnp.where(is_new, new_tile(nv_ref, SEL_BLOCK), cv_ref[...])
    kpos = blk * SEL_BLOCK + lane
    mask = jnp.logical_and(jnp.logical_and(blk >= 0, kpos <= qpos), kpos < past + n_new)
    step(k, v, mask)

    @pl.when(n == n_top - 1)
    def _():
        osel_ref[...] = acc_ref[...] / jnp.maximum(l_ref[...], TINY)
        reset()
        wlane = lax.broadcasted_iota(jnp.int32, (ROWS_PAD, win_buf), 1)
        wpos = past - win_buf + wlane
        diff = qpos - wpos
        step(wk_ref[...], wv_ref[...], jnp.logical_and(jnp.logical_and(diff >= 0, diff <= WINDOW), wpos >= 0))
        nlane = lax.broadcasted_iota(jnp.int32, (ROWS_PAD, LANES), 1)
        ndiff = qpos - (past + nlane)
        step(new_tile(nwk_ref, LANES), new_tile(nwv_ref, LANES),
             jnp.logical_and(jnp.logical_and(ndiff >= 0, ndiff <= WINDOW), nlane < n_new))
        o_win = acc_ref[...] / jnp.maximum(l_ref[...], TINY)
        gates = gate_ref[...]
        o_ref[...] = gates[:, 0:1] * ocmp_ref[...] + gates[:, 1:2] * osel_ref[...] + gates[:, 2:3] * o_win


def _nsa_main_sample(idx, page_table, q_rot, cache, rows_new, win_state, win_new, o_cmp, gates, *, n_top):
    b, n_pages = page_table.shape
    past = n_pages * PAGE_SIZE
    win_buf = win_state.shape[1]
    per_page = PAGE_SIZE // SEL_BLOCK
    kh = NSA_KV_HEADS

    def sel_spec(slot):
        def index(bi, g, n, idx_ref, pt):
            blk = jnp.maximum(idx_ref[(bi * kh + g) * n_top + n], 0)
            page = jnp.minimum(blk // per_page, n_pages - 1)
            return pt[bi, page], blk % per_page, slot * kh + g
        return pl.BlockSpec((None, SEL_BLOCK, HEAD_DIM), index)

    def col_spec(rows, slot):
        return pl.BlockSpec((None, rows, HEAD_DIM), lambda bi, g, n, idx_ref, pt: (bi, 0, slot * kh + g))

    head_spec = pl.BlockSpec((None, None, ROWS_PAD, HEAD_DIM), lambda bi, g, n, idx_ref, pt: (bi, g, 0, 0))
    grid_spec = pltpu.PrefetchScalarGridSpec(
        num_scalar_prefetch=2,
        grid=(b, kh, n_top),
        in_specs=[
            head_spec,
            sel_spec(2), sel_spec(3),
            col_spec(1, 2), col_spec(1, 3),
            col_spec(win_buf, 0), col_spec(win_buf, 1),
            col_spec(1, 0), col_spec(1, 1),
            head_spec, head_spec,
        ],
        out_specs=head_spec,
        scratch_shapes=[
            pltpu.VMEM((ROWS_PAD, 1), F32),
            pltpu.VMEM((ROWS_PAD, 1), F32),
            pltpu.VMEM((ROWS_PAD, HEAD_DIM), F32),
            pltpu.VMEM((ROWS_PAD, HEAD_DIM), F32),
        ],
    )
    return pl.pallas_call(
        functools.partial(_nsa_main_sample_kernel, n_top=n_top, past=past, n_new=1),
        grid_spec=grid_spec,
        out_shape=jax.ShapeDtypeStruct((b, kh, ROWS_PAD, HEAD_DIM), F32),
        compiler_params=_cparams(("parallel", "parallel", "arbitrary")),
        name="nsa_main_sample",
    )(idx, page_table, q_rot, cache, cache, rows_new, rows_new, win_state, win_state, win_new, win_new, o_cmp, gates)


def _merge_kernel(osb_ref, onsa_ref, g0_ref, g1_ref, pa_ref, pb_ref, wo_ref, x_ref, out_ref, acc_ref):
    j = pl.program_id(1)

    @pl.when(j == 0)
    def _():
        acc_ref[...] = jnp.zeros_like(acc_ref)

    a = jnp.dot(osb_ref[...], pa_ref[...], preferred_element_type=F32)
    b = jnp.dot(onsa_ref[...], pb_ref[...], preferred_element_type=F32)
    m = (g0_ref[...] * a + g1_ref[...] * b).astype(BF16)
    acc_ref[...] += jnp.dot(m, wo_ref[...], preferred_element_type=F32)

    @pl.when(j == pl.num_programs(1) - 1)
    def _():
        out_ref[...] = x_ref[...] + acc_ref[...]


def _merge(o_sb, o_nsa, mg, pa, pb, wo, x, *, tm_pref=512, tn=512):
    m, d = x.shape
    ka, kb = o_sb.shape[1], o_nsa.shape[1]
    tm = _pick_tile(m, tm_pref)
    tn = _pick_tile(d, tn)
    nj = d // tn
    return pl.pallas_call(
        _merge_kernel,
        grid=(m // tm, nj),
        in_specs=[
            pl.BlockSpec((tm, ka), lambda i, j: (i, 0)),
            pl.BlockSpec((tm, kb), lambda i, j: (i, 0)),
            pl.BlockSpec((tm, tn), lambda i, j: (i, j)),
            pl.BlockSpec((tm, tn), lambda i, j: (i, nj + j)),
            pl.BlockSpec((ka, tn), lambda i, j: (0, j)),
            pl.BlockSpec((kb, tn), lambda i, j: (0, j)),
            pl.BlockSpec((tn, d), lambda i, j: (j, 0)),
            pl.BlockSpec((tm, d), lambda i, j: (i, 0)),
        ],
        out_specs=pl.BlockSpec((tm, d), lambda i, j: (i, 0)),
        out_shape=jax.ShapeDtypeStruct((m, d), F32),
        scratch_shapes=[pltpu.VMEM((tm, d), F32)],
        compiler_params=_cparams(("parallel", "arbitrary")),
        name="merge",
    )(o_sb, o_nsa, mg, mg, pa, pb, wo, x)


SB_Q_COLS = SB_HEADS * HEAD_DIM
SB_KV_COLS = 2 * SB_HEADS * HEAD_DIM
NSA_Q_COLS = NSA_Q_HEADS * HEAD_DIM
NSA_ROW_COLS = 4 * NSA_KV_HEADS * HEAD_DIM
NSA_WIN_COLS = 2 * NSA_KV_HEADS * HEAD_DIM
NSA_GATE_COLS = 3 * NSA_Q_HEADS


def _prep_layer_weights(layer, g_ffn1, ffn1_w_gu, ffn1_w_down, g_mix, w_in, cmp_pe_k, cmp_w_k, cmp_pe_v, cmp_w_v,
                        p_a, p_b, w_o, g_ffn2, ffn2_w_gu, ffn2_w_down):
    w = {}
    w["g1"], w["g_mix"], w["g2"] = g_ffn1[layer][None], g_mix[layer][None], g_ffn2[layer][None]
    w["ffn1"] = _prep_ffn_weights(ffn1_w_gu[layer], ffn1_w_down[layer])
    w["ffn2"] = _prep_ffn_weights(ffn2_w_gu[layer], ffn2_w_down[layer])
    wi = w_in[layer]
    off = 0
    for name, n in (("sbq", SB_Q_COLS), ("sbkv", SB_KV_COLS), ("nq", NSA_Q_COLS), ("rows", NSA_ROW_COLS),
                    ("win", NSA_WIN_COLS)):
        w[name] = wi[:, off:off + n].astype(BF16)
        off += n
    wg = wi[:, off:off + NSA_GATE_COLS].reshape(-1, 3, NSA_KV_HEADS, NSA_GROUP).transpose(0, 2, 1, 3)
    wg = wg.reshape(-1, NSA_KV_HEADS, 3 * NSA_GROUP)
    wg = jnp.pad(wg, ((0, 0), (0, 0), (0, LANES - 3 * NSA_GROUP)))
    w["gate"] = wg.reshape(-1, NSA_KV_HEADS * LANES).astype(BF16)
    off += NSA_GATE_COLS
    w["mg"] = wi[:, off:].astype(BF16)
    w["pe"] = jnp.stack([cmp_pe_k[layer], cmp_pe_v[layer]])
    w["wc"] = jnp.stack([cmp_w_k[layer], cmp_w_v[layer]]).astype(BF16)
    w["pa"], w["pb"], w["wo"] = p_a[layer].astype(BF16), p_b[layer].astype(BF16), w_o[layer].astype(BF16)
    return w


def _mixer_inputs(h, w, cos2, sin2):
    plain_bf = ("plain", None, BF16)
    (sbq,) = _proj(h, w["sbq"], (plain_bf,))
    kv_f, kv_b = _proj(h, w["sbkv"], (("plain", None, F32), plain_bf))
    nq_raw, nq_rot = _proj(h, w["nq"], (plain_bf, ("rope", None, BF16)), cos2, sin2)
    rows_f, rows_b = _proj(h, w["rows"], (("rope", (2,), F32), ("rope", (2,), BF16)), cos2, sin2)
    win_f, win_b = _proj(h, w["win"], (("rope", (0,), F32), ("rope", (0,), BF16)), cos2, sin2)
    (gates,) = _proj(h, w["gate"], (("sigmoid", None, F32),), tn=LANES)
    (mg,) = _proj(h, w["mg"], (("sigmoid", None, F32),))
    return dict(sbq=sbq, kv_f=kv_f, kv_b=kv_b, nq_raw=nq_raw, nq_rot=nq_rot, rows_f=rows_f, rows_b=rows_b,
                win_f=win_f, win_b=win_b, gates=gates, mg=mg)


def _prompt_layer(x, w, cos2, sin2, g_next, last):
    t = x.shape[0]
    assert t % LANES == 0 and t >= CMP_BLOCK
    x1, h = _ffn(x, w["g1"], *w["ffn1"], w["g_mix"], emit_x=True, norm_dtype=BF16)
    mi = _mixer_inputs(h, w, cos2, sin2)
    o_sb = _sb_prompt(mi["sbq"], mi["kv_b"])
    p0, p1 = _compress_prompt(mi["rows_f"], w["pe"], w["wc"])
    n_cmp = (t - CMP_BLOCK) // CMP_STRIDE + 1
    n_sel = -(-t // SEL_BLOCK)
    tq = _pick_tile(t, 128)
    o_cmp, sel, _ = _nsa_cmp(mi["nq_raw"][None], p0, p1, tq=tq, qpos0=0, n_cmp=n_cmp, n_sel=n_sel)
    o_nsa = _nsa_main_prompt(mi["nq_rot"], mi["win_b"], mi["rows_b"], sel[0], o_cmp[0], mi["gates"])
    x2 = _merge(o_sb, o_nsa, mi["mg"], w["pa"], w["pb"], w["wo"], x1)
    if last:
        (y,) = _ffn(x2, w["g2"], *w["ffn2"], g_next, emit_x=False, norm_dtype=F32)
        x3 = None
    else:
        x3, y = _ffn(x2, w["g2"], *w["ffn2"], g_next, emit_x=True, norm_dtype=F32)
    return x3, y, mi["kv_f"], mi["rows_f"], mi["win_f"]


def _pad_rows(a, rows):
    return jnp.pad(a, ((0, 0), (0, 0), (0, rows - a.shape[2]), (0, 0)))


def _sample_layer(x, w, cos2, sin2, cache_sb, cache_nsa, win_state, page_table, g_next, last):
    b = x.shape[0]
    n_pages = page_table.shape[1]
    past = n_pages * PAGE_SIZE
    total = past + 1
    kh, grp = NSA_KV_HEADS, NSA_GROUP
    x1, h = _ffn(x, w["g1"], *w["ffn1"], w["g_mix"], emit_x=True, norm_dtype=BF16)
    mi = _mixer_inputs(h, w, cos2, sin2)
    o_sb = _sb_sample(mi["sbq"].astype(F32), mi["kv_f"], cache_sb, page_table)

    p0, p1 = _compress_sample(cache_nsa, page_table, w["pe"], w["wc"])
    n_cmp = (total - CMP_BLOCK) // CMP_STRIDE + 1
    n_sel = -(-total // SEL_BLOCK)
    n_top = min(SEL_TOPN, n_sel)
    q_raw = jnp.pad(mi["nq_raw"][:, None, :], ((0, 0), (0, ROWS_PAD - 1), (0, 0)))
    o_cmp, _, idx = _nsa_cmp(q_raw, p0, p1, tq=ROWS_PAD, qpos0=past, n_cmp=n_cmp, n_sel=n_sel)
    idx = idx[:, :, 0, :n_top].reshape(-1)
    q_rot = _pad_rows(mi["nq_rot"].reshape(b, kh, grp, HEAD_DIM), ROWS_PAD)
    o_cmp = _pad_rows(o_cmp[:, 0, :].reshape(b, kh, grp, HEAD_DIM), ROWS_PAD)
    gates = mi["gates"].reshape(b, kh, LANES)[:, :, :3 * grp].reshape(b, kh, 3, grp).transpose(0, 1, 3, 2)
    gates = jnp.pad(gates, ((0, 0), (0, 0), (0, ROWS_PAD - grp), (0, LANES - 3)))
    o_nsa = _nsa_main_sample(idx, page_table, q_rot, cache_nsa, mi["rows_f"][:, None, :], win_state,
                             mi["win_f"][:, None, :], o_cmp, gates, n_top=n_top)
    o_nsa = o_nsa[:, :, :grp, :].reshape(b, NSA_Q_HEADS * HEAD_DIM)

    x2 = _merge(o_sb.astype(BF16), o_nsa.astype(BF16), mi["mg"], w["pa"], w["pb"], w["wo"], x1)
    if last:
        (y,) = _ffn(x2, w["g2"], *w["ffn2"], g_next, emit_x=False, norm_dtype=F32)
        x3 = None
    else:
        x3, y = _ffn(x2, w["g2"], *w["ffn2"], g_next, emit_x=True, norm_dtype=F32)
    return x3, y, mi["kv_f"], mi["rows_f"], mi["win_f"]


def kernel(x_prompt, x_sample, cache_sb_kv, cache_nsa_kv, state_win_kv, page_table, g_ffn1, ffn1_w_gu, ffn1_w_down,
           g_mix, w_in, cmp_pe_k, cmp_w_k, cmp_pe_v, cmp_w_v, p_a, p_b, w_o, g_ffn2, ffn2_w_gu, ffn2_w_down, g_final):
    bp, t, d = x_prompt.shape
    db, ds, _ = x_sample.shape
    depth = g_ffn1.shape[0]
    n_pool = cache_sb_kv.shape[1]
    assert bp == 1 and ds == 1
    past = page_table.shape[1] * PAGE_SIZE
    win_buf = state_win_kv.shape[2]
    cos_p, sin_p = _rope_tables(jnp.arange(t))
    cos_s, sin_s = _rope_tables(jnp.full((db,), past, jnp.int32))
    g_fin = g_final[None]

    xp = x_prompt.reshape(t, d)
    xs = x_sample.reshape(db, d)
    sb_p, nsa_p, win_p, sb_s, nsa_s, win_s = [], [], [], [], [], []
    yp = ys = None
    for layer in range(depth):
        w = _prep_layer_weights(layer, g_ffn1, ffn1_w_gu, ffn1_w_down, g_mix, w_in, cmp_pe_k, cmp_w_k, cmp_pe_v,
                                cmp_w_v, p_a, p_b, w_o, g_ffn2, ffn2_w_gu, ffn2_w_down)
        last = layer == depth - 1
        xp, yp, kv_f, rows_f, win_f = _prompt_layer(xp, w, cos_p, sin_p, g_fin, last)
        sb_p.append(kv_f.reshape(1, t, 2, SB_HEADS, HEAD_DIM))
        nsa_p.append(rows_f.reshape(1, t, 4, NSA_KV_HEADS, HEAD_DIM))
        wp = min(WINDOW, t)
        win_p.append(win_f[t - wp:].reshape(1, wp, 2, NSA_KV_HEADS, HEAD_DIM))

        cache_sb = cache_sb_kv[layer].reshape(n_pool, PAGE_SIZE, 2 * SB_HEADS * HEAD_DIM)
        cache_nsa = cache_nsa_kv[layer].reshape(n_pool, PAGE_SIZE, 4 * NSA_KV_HEADS * HEAD_DIM)
        win_state = state_win_kv[layer].reshape(db, win_buf, 2 * NSA_KV_HEADS * HEAD_DIM)
        xs, ys, kv_f, rows_f, win_f = _sample_layer(xs, w, cos_s, sin_s, cache_sb, cache_nsa, win_state, page_table,
                                                    g_fin, last)
        sb_s.append(kv_f.reshape(db, 1, 2, SB_HEADS, HEAD_DIM))
        nsa_s.append(rows_f.reshape(db, 1, 4, NSA_KV_HEADS, HEAD_DIM))
        win_all = jnp.concatenate([state_win_kv[layer], win_f.reshape(db, 1, 2, NSA_KV_HEADS, HEAD_DIM)], axis=1)
        win_s.append(win_all[:, win_all.shape[1] - win_buf:])
    return (yp.reshape(1, t, d), ys.reshape(db, 1, d), jnp.stack(sb_p), jnp.stack(nsa_p), jnp.stack(win_p),
            jnp.stack(sb_s), jnp.stack(nsa_s), jnp.stack(win_s))
```

```python
import functools

import numpy as np
import jax
import jax.numpy as jnp
from jax import lax
from jax.experimental import pallas as pl
from jax.experimental.pallas import tpu as pltpu

HEAD_DIM = 128
SB_HEADS = 8
NSA_Q_HEADS = 8
NSA_KV_HEADS = 2
NSA_GROUP = NSA_Q_HEADS // NSA_KV_HEADS
CMP_BLOCK = 32
CMP_STRIDE = 16
SEL_BLOCK = 64
SEL_TOPN = 16
WINDOW = 512
PAGE_SIZE = 128
ROPE_THETA = 10000.0
NORM_EPS = 1e-6
HALF_STEP = 0.5
SCALE = HEAD_DIM ** -0.5
NEG_INF = -1e30
FORCED_SCORE = 1e4
TINY = 1e-30

SB_DEAD_LOG = -104.0

LANES = 128
SB_CACHE_SLOTS = 2 * SB_HEADS
NSA_CACHE_SLOTS = 4 * NSA_KV_HEADS
WIN_CACHE_SLOTS = 2 * NSA_KV_HEADS
VMEM_LIMIT = 56 * 1024 * 1024

F32 = jnp.float32
BF16 = jnp.bfloat16


def _cparams(sem):
    return pltpu.CompilerParams(dimension_semantics=sem, vmem_limit_bytes=VMEM_LIMIT)


def _round_up(x, m):
    return -(-x // m) * m


def _pick_tile(n, pref):
    t = min(pref, n)
    while n % t:
        t //= 2
    return t


def _rms(x, g):
    return x * lax.rsqrt(jnp.mean(x * x, axis=-1, keepdims=True) + NORM_EPS) * g


def _sigmoid(x):
    return 1.0 / (1.0 + jnp.exp(-x))


def _ffn_kernel(x_ref, g_ref, wg_ref, wu_ref, wd_ref, g2_ref, *refs, emit_x):
    if emit_x:
        out_ref, n_ref, h_scr, acc_scr = refs
    else:
        n_ref, h_scr, acc_scr = refs
    j = pl.program_id(1)

    @pl.when(j == 0)
    def _():
        h_scr[...] = _rms(x_ref[...], g_ref[...]).astype(BF16)
        acc_scr[...] = jnp.zeros_like(acc_scr)

    h = h_scr[...]
    gate = jnp.dot(h, wg_ref[...], preferred_element_type=F32)
    up = jnp.dot(h, wu_ref[...], preferred_element_type=F32)
    act = (gate * _sigmoid(gate) * up).astype(BF16)
    acc_scr[...] += jnp.dot(act, wd_ref[...], preferred_element_type=F32)

    @pl.when(j == pl.num_programs(1) - 1)
    def _():
        y = x_ref[...] + HALF_STEP * acc_scr[...]
        if emit_x:
            out_ref[...] = y
        n_ref[...] = _rms(y, g2_ref[...]).astype(n_ref.dtype)


def _ffn(x, g, wg, wu, wd, g2, *, emit_x, norm_dtype, tm_pref=512, tf=512):
    m, d = x.shape
    fpad = wg.shape[1]
    tm = _pick_tile(m, tm_pref)
    grid = (m // tm, fpad // tf)
    out_shape = []
    out_specs = []
    if emit_x:
        out_shape.append(jax.ShapeDtypeStruct((m, d), F32))
        out_specs.append(pl.BlockSpec((tm, d), lambda i, j: (i, 0)))
    out_shape.append(jax.ShapeDtypeStruct((m, d), norm_dtype))
    out_specs.append(pl.BlockSpec((tm, d), lambda i, j: (i, 0)))
    return pl.pallas_call(
        functools.partial(_ffn_kernel, emit_x=emit_x),
        grid=grid,
        in_specs=[
            pl.BlockSpec((tm, d), lambda i, j: (i, 0)),
            pl.BlockSpec((1, d), lambda i, j: (0, 0)),
            pl.BlockSpec((d, tf), lambda i, j: (0, j)),
            pl.BlockSpec((d, tf), lambda i, j: (0, j)),
            pl.BlockSpec((tf, d), lambda i, j: (j, 0)),
            pl.BlockSpec((1, d), lambda i, j: (0, 0)),
        ],
        out_specs=out_specs,
        out_shape=out_shape,
        scratch_shapes=[pltpu.VMEM((tm, d), BF16), pltpu.VMEM((tm, d), F32)],
        compiler_params=_cparams(("parallel", "arbitrary")),
        name="ffn",
    )(x, g, wg, wu, wd, g2)


def _prep_ffn_weights(w_gu, w_down, tf=512):
    d, two_f = w_gu.shape
    f = two_f // 2
    fpad = _round_up(f, tf)
    wg = jnp.pad(w_gu[:, :f].astype(BF16), ((0, 0), (0, fpad - f)))
    wu = jnp.pad(w_gu[:, f:].astype(BF16), ((0, 0), (0, fpad - f)))
    wd = jnp.pad(w_down.astype(BF16), ((0, fpad - f), (0, 0)))
    return wg, wu, wd


def _rope_tile(x, cos2, sin2):
    return x * cos2 + pltpu.roll(x, HEAD_DIM // 2, axis=1) * sin2


def _proj_kernel(a_ref, w_ref, *refs, outs, use_rope):
    if use_rope:
        cos_ref, sin_ref = refs[:2]
        refs = refs[2:]
    j = pl.program_id(1)
    acc = jnp.dot(a_ref[...], w_ref[...], preferred_element_type=F32)
    tn = acc.shape[1]
    for (mode, rope_tiles, _), o_ref in zip(outs, refs):
        if mode == "sigmoid":
            o_ref[...] = _sigmoid(acc).astype(o_ref.dtype)
        elif mode == "rope":
            cos2 = cos_ref[...]
            sin2 = sin_ref[...]
            rot = jnp.concatenate(
                [_rope_tile(acc[:, c * LANES:(c + 1) * LANES], cos2, sin2) for c in range(tn // LANES)], axis=1)
            if rope_tiles is None:
                o_ref[...] = rot.astype(o_ref.dtype)
            else:
                is_rope = functools.reduce(jnp.logical_or, [j == t for t in rope_tiles])
                o_ref[...] = jnp.where(is_rope, rot, acc).astype(o_ref.dtype)
        else:
            o_ref[...] = acc.astype(o_ref.dtype)


def _proj(a, w, outs, cos2=None, sin2=None, *, tm_pref=1024, tn=256):
    m, k = a.shape
    n = w.shape[1]
    tm = _pick_tile(m, tm_pref)
    tn = min(tn, n)
    use_rope = any(mode == "rope" for mode, _, _ in outs)
    in_specs = [pl.BlockSpec((tm, k), lambda i, j: (i, 0)), pl.BlockSpec((k, tn), lambda i, j: (0, j))]
    args = [a, w]
    if use_rope:
        in_specs += [pl.BlockSpec((tm, LANES), lambda i, j: (i, 0))] * 2
        args += [cos2, sin2]
    res = pl.pallas_call(
        functools.partial(_proj_kernel, outs=outs, use_rope=use_rope),
        grid=(m // tm, n // tn),
        in_specs=in_specs,
        out_specs=[pl.BlockSpec((tm, tn), lambda i, j: (i, j)) for _ in outs],
        out_shape=[jax.ShapeDtypeStruct((m, n), dt) for _, _, dt in outs],
        compiler_params=_cparams(("parallel", "arbitrary")),
        name="proj",
    )(*args)
    return res


def _rope_tables(pos):
    half = HEAD_DIM // 2
    inv_freq = ROPE_THETA ** (-2.0 * jnp.arange(half, dtype=F32) / HEAD_DIM)
    ang = pos.astype(F32)[:, None] * inv_freq[None, :]
    cos, sin = jnp.cos(ang), jnp.sin(ang)
    return jnp.concatenate([cos, cos], axis=1), jnp.concatenate([-sin, sin], axis=1)


def _later_matrix(tk):
    j = np.arange(tk)[:, None]
    s = np.arange(tk)[None, :]
    return jnp.asarray((j > s).astype(np.float32), dtype=BF16)


def _sb_tile(z, u, carry, mask):
    soft = jnp.log(1.0 + jnp.exp(-jnp.abs(z)))
    log_beta = jnp.minimum(z, 0.0) - soft
    log_keep = jnp.where(mask, jnp.minimum(-z, 0.0) - soft, 0.0)
    hi = log_keep.astype(BF16)
    lo = (log_keep - hi.astype(F32)).astype(BF16)
    later = (jnp.dot(hi, u, preferred_element_type=F32) + jnp.dot(lo, u, preferred_element_type=F32)) + carry
    a = jnp.where(mask, jnp.exp(log_beta + later), 0.0)
    return a.astype(BF16), carry + jnp.sum(log_keep, axis=1, keepdims=True)


def _sb_prompt_kernel(q_ref, k_ref, v_ref, u_ref, o_ref, acc_ref, carry_ref, *, tq, tk):
    qi = pl.program_id(1)
    acc_ref[...] = jnp.zeros_like(acc_ref)
    carry_ref[...] = jnp.zeros_like(carry_ref)
    q = q_ref[...]
    u = u_ref[...]
    qpos = qi * tq + lax.broadcasted_iota(jnp.int32, (tq, tk), 0)
    lane = lax.broadcasted_iota(jnp.int32, (tq, tk), 1)

    def cond(state):
        j, done = state
        return jnp.logical_and(j >= 0, done == 0)

    def body(state):
        j, _ = state
        start = pl.multiple_of(j * tk, tk)
        k = k_ref[pl.ds(start, tk), :]
        v = v_ref[pl.ds(start, tk), :]
        z = lax.dot_general(q, k, (((1,), (1,)), ((), ())), preferred_element_type=F32) * SCALE
        mask = (start + lane) < qpos
        a, carry = _sb_tile(z, u, carry_ref[...], mask)
        acc_ref[...] += jnp.dot(a, v, preferred_element_type=F32)
        carry_ref[...] = carry
        done = (jnp.max(carry) < SB_DEAD_LOG).astype(jnp.int32)
        return j - 1, done

    n_tiles = (qi + 1) * (tq // tk)
    lax.while_loop(cond, body, (n_tiles - 1, jnp.int32(0)))
    o_ref[...] = acc_ref[...].astype(o_ref.dtype)


def _sb_prompt(q_bf, kv_bf, *, tq=256, tk=256):
    t = q_bf.shape[0]
    tq = _pick_tile(t, tq)
    tk = min(tk, tq)
    return pl.pallas_call(
        functools.partial(_sb_prompt_kernel, tq=tq, tk=tk),
        grid=(SB_HEADS, t // tq),
        in_specs=[
            pl.BlockSpec((tq, HEAD_DIM), lambda h, i: (i, h)),
            pl.BlockSpec((t, HEAD_DIM), lambda h, i: (0, h)),
            pl.BlockSpec((t, HEAD_DIM), lambda h, i: (0, SB_HEADS + h)),
            pl.BlockSpec((tk, tk), lambda h, i: (0, 0)),
        ],
        out_specs=pl.BlockSpec((tq, HEAD_DIM), lambda h, i: (i, h)),
        out_shape=jax.ShapeDtypeStruct((t, SB_HEADS * HEAD_DIM), BF16),
        scratch_shapes=[pltpu.VMEM((tq, HEAD_DIM), F32), pltpu.VMEM((tq, 1), F32)],
        compiler_params=_cparams(("parallel", "arbitrary")),
        name="sb_prompt",
    )(q_bf, kv_bf, kv_bf, _later_matrix(tk))


def _compress_kernel(*refs, n_in, rows, prefetch):
    if prefetch:
        refs = refs[1:]
    in_refs = refs[:n_in]
    pe_ref, w_ref, p0_ref, p1_ref, col_ref = refs[n_in:]
    nch = rows // CMP_STRIDE
    for sg in range(2 * NSA_KV_HEADS):
        slot = sg // NSA_KV_HEADS
        for i, r in enumerate(in_refs):
            if prefetch:
                col_ref[i * rows:(i + 1) * rows, :] = r[pl.ds(sg, rows, stride=NSA_CACHE_SLOTS), :]
            else:
                col_ref[i * rows:(i + 1) * rows, :] = r[:, sg * LANES:(sg + 1) * LANES]
        acc0 = jnp.zeros((n_in * nch, HEAD_DIM), F32)
        acc1 = jnp.zeros((n_in * nch, HEAD_DIM), F32)
        for j in range(CMP_STRIDE):
            x = col_ref[pl.ds(j, n_in * nch, stride=CMP_STRIDE), :]
            x0 = (x + pe_ref[slot, pl.ds(j, 1), :]).astype(BF16)
            x1 = (x + pe_ref[slot, pl.ds(CMP_STRIDE + j, 1), :]).astype(BF16)
            acc0 += jnp.dot(x0, w_ref[slot, j], preferred_element_type=F32)
            acc1 += jnp.dot(x1, w_ref[slot, CMP_STRIDE + j], preferred_element_type=F32)
        p0_ref[sg] = acc0
        p1_ref[sg] = acc1


def _compress_prompt(rows_f32, pe, w, *, rows_pref=2048):
    t = rows_f32.shape[0]
    rows = _pick_tile(t, rows_pref)
    nch = rows // CMP_STRIDE
    n_sg = 2 * NSA_KV_HEADS
    out_sds = jax.ShapeDtypeStruct((1, n_sg, t // CMP_STRIDE, HEAD_DIM), F32)
    out_spec = pl.BlockSpec((None, n_sg, nch, HEAD_DIM), lambda i: (0, 0, i, 0))
    return pl.pallas_call(
        functools.partial(_compress_kernel, n_in=1, rows=rows, prefetch=False),
        grid=(t // rows,),
        in_specs=[
            pl.BlockSpec((rows, n_sg * LANES), lambda i: (i, 0)),
            pl.BlockSpec(pe.shape, lambda i: (0, 0, 0)),
            pl.BlockSpec(w.shape, lambda i: (0, 0, 0, 0)),
        ],
        out_specs=[out_spec, out_spec],
        out_shape=[out_sds, out_sds],
        scratch_shapes=[pltpu.VMEM((rows, HEAD_DIM), F32)],
        compiler_params=_cparams(("parallel",)),
        name="compress_prompt",
    )(rows_f32, pe, w)


def _compress_sample(cache, page_table, pe, w, *, pages_per_step=16):
    b, n_pages = page_table.shape
    npp = _pick_tile(n_pages, pages_per_step)
    nch = PAGE_SIZE // CMP_STRIDE
    n_sg = 2 * NSA_KV_HEADS
    out_sds = jax.ShapeDtypeStruct((b, n_sg, n_pages * nch, HEAD_DIM), F32)
    out_spec = pl.BlockSpec((None, n_sg, npp * nch, HEAD_DIM), lambda bi, i, pt: (bi, 0, i, 0))

    def page_spec(k):
        return pl.BlockSpec((None, PAGE_SIZE * NSA_CACHE_SLOTS, HEAD_DIM),
                            lambda bi, i, pt: (pt[bi, i * npp + k], 0, 0))

    grid_spec = pltpu.PrefetchScalarGridSpec(
        num_scalar_prefetch=1,
        grid=(b, n_pages // npp),
        in_specs=[page_spec(k) for k in range(npp)] + [
            pl.BlockSpec(pe.shape, lambda bi, i, pt: (0, 0, 0)),
            pl.BlockSpec(w.shape, lambda bi, i, pt: (0, 0, 0, 0)),
        ],
        out_specs=[out_spec, out_spec],
        scratch_shapes=[pltpu.VMEM((npp * PAGE_SIZE, HEAD_DIM), F32)],
    )
    return pl.pallas_call(
        functools.partial(_compress_kernel, n_in=npp, rows=PAGE_SIZE, prefetch=True),
        grid_spec=grid_spec,
        out_shape=[out_sds, out_sds],
        compiler_params=_cparams(("parallel", "arbitrary")),
        name="compress_sample",
    )(page_table, *([cache] * npp), pe, w)


def _selection_matrix(n_c, n_cmp, n_sel, n_sel_pad):
    ratio = SEL_BLOCK // CMP_STRIDE
    lo = -((CMP_BLOCK - 1) // CMP_STRIDE)
    hi = (SEL_BLOCK - 1) // CMP_STRIDE
    c = np.arange(n_c)[:, None]
    b = np.arange(n_sel_pad)[None, :]
    hit = (c >= ratio * b + lo) & (c <= ratio * b + hi) & (c < n_cmp) & (b < n_sel)
    return jnp.asarray(hit.astype(np.float32), dtype=BF16)


def _split_dot(x, m01):
    hi = x.astype(BF16)
    lo = (x - hi.astype(F32)).astype(BF16)
    return jnp.dot(hi, m01, preferred_element_type=F32) + jnp.dot(lo, m01, preferred_element_type=F32)


def _nsa_cmp_kernel(q_ref, p0k_ref, p1k_ref, p0v_ref, p1v_ref, smap_ref, ocmp_ref, sel_ref, idx_ref,
                    *, tq, qpos0, n_sel, n_top, sel_t):
    qi = pl.program_id(2)
    n_c = p0k_ref.shape[0]
    n_sel_pad = smap_ref.shape[1]
    ck = (p0k_ref[...] + pltpu.roll(p1k_ref[...], n_c - 1, axis=0)).astype(BF16)
    cv = (p0v_ref[...] + pltpu.roll(p1v_ref[...], n_c - 1, axis=0)).astype(BF16)
    q = q_ref[...]
    qs = jnp.concatenate([q[:, r * LANES:(r + 1) * LANES] for r in range(NSA_GROUP)], axis=0)
    s = lax.dot_general(qs, ck, (((1,), (1,)), ((), ())), preferred_element_type=F32) * SCALE
    qpos_c = qpos0 + qi * tq + lax.broadcasted_iota(jnp.int32, (tq, n_c), 0)
    cend = lax.broadcasted_iota(jnp.int32, (tq, n_c), 1) * CMP_STRIDE + (CMP_BLOCK - 1)
    m1 = cend <= qpos_c
    mask = jnp.concatenate([m1] * NSA_GROUP, axis=0)
    s = jnp.where(mask, s, NEG_INF)
    p = jnp.where(mask, jnp.exp(s - jnp.max(s, axis=1, keepdims=True)), 0.0)
    p = p / jnp.maximum(jnp.sum(p, axis=1, keepdims=True), TINY)
    o = jnp.dot(p.astype(BF16), cv, preferred_element_type=F32)
    ocmp_ref[...] = jnp.concatenate([o[r * tq:(r + 1) * tq] for r in range(NSA_GROUP)], axis=1)

    imp = p[0:tq]
    for r in range(1, NSA_GROUP):
        imp = imp + p[r * tq:(r + 1) * tq]
    imp_sel = _split_dot(imp, smap_ref[...])
    qpos = qpos0 + qi * tq + lax.broadcasted_iota(jnp.int32, (tq, n_sel_pad), 0)
    blk = lax.broadcasted_iota(jnp.int32, (tq, n_sel_pad), 1)
    cur = qpos // SEL_BLOCK
    eligible = jnp.logical_and(blk * SEL_BLOCK <= qpos, blk < n_sel)
    forced = jnp.logical_or(blk == 0, jnp.logical_or(blk == cur, blk == cur - 1))
    score = jnp.where(eligible, jnp.where(forced, FORCED_SCORE, imp_sel), NEG_INF)
    blkf = blk.astype(F32)
    lane = lax.broadcasted_iota(jnp.int32, (tq, LANES), 1)
    selected = jnp.zeros((tq, n_sel_pad), F32)
    idx_acc = jnp.full((tq, LANES), -1.0, F32)
    for i in range(n_top):
        mx = jnp.max(score, axis=1, keepdims=True)
        first = jnp.min(jnp.where(score == mx, blkf, 1e9), axis=1, keepdims=True)
        valid = mx > 0.5 * NEG_INF
        hit = blkf == first
        selected = jnp.where(jnp.logical_and(hit, valid), 1.0, selected)
        idx_acc = jnp.where(lane == i, jnp.where(valid, first, -1.0), idx_acc)
        score = jnp.where(hit, -3e38, score)
    if sel_t:
        sel_ref[...] = selected.T.astype(sel_ref.dtype)
    else:
        sel_ref[...] = selected.astype(sel_ref.dtype)
    idx_ref[...] = idx_acc.astype(jnp.int32)


def _nsa_cmp(q_bf, p0, p1, *, tq, qpos0, n_cmp, n_sel, sel_t):
    b, t, _ = q_bf.shape
    n_c = p0.shape[2]
    n_sel_pad = _round_up(n_sel, LANES)
    n_top = min(SEL_TOPN, n_sel)
    smap = _selection_matrix(n_c, n_cmp, n_sel, n_sel_pad)
    gw = NSA_GROUP * HEAD_DIM

    def part_spec(slot):
        return pl.BlockSpec((None, None, n_c, HEAD_DIM), lambda bi, g, i: (bi, slot * NSA_KV_HEADS + g, 0, 0))

    if sel_t:
        sel_spec = pl.BlockSpec((None, None, n_sel_pad, tq), lambda bi, g, i: (bi, g, 0, i))
        sel_sds = jax.ShapeDtypeStruct((b, NSA_KV_HEADS, n_sel_pad, t), BF16)
    else:
        sel_spec = pl.BlockSpec((None, None, tq, n_sel_pad), lambda bi, g, i: (bi, g, i, 0))
        sel_sds = jax.ShapeDtypeStruct((b, NSA_KV_HEADS, t, n_sel_pad), F32)
    return pl.pallas_call(
        functools.partial(_nsa_cmp_kernel, tq=tq, qpos0=qpos0, n_sel=n_sel, n_top=n_top, sel_t=sel_t),
        grid=(b, NSA_KV_HEADS, t // tq),
        in_specs=[
            pl.BlockSpec((None, tq, gw), lambda bi, g, i: (bi, i, g)),
            part_spec(0), part_spec(0), part_spec(1), part_spec(1),
            pl.BlockSpec((n_c, n_sel_pad), lambda bi, g, i: (0, 0)),
        ],
        out_specs=[
            pl.BlockSpec((None, tq, gw), lambda bi, g, i: (bi, i, g)),
            sel_spec,
            pl.BlockSpec((None, None, tq, LANES), lambda bi, g, i: (bi, g, i, 0)),
        ],
        out_shape=[
            jax.ShapeDtypeStruct((b, t, NSA_Q_HEADS * HEAD_DIM), F32),
            sel_sds,
            jax.ShapeDtypeStruct((b, NSA_KV_HEADS, t, LANES), jnp.int32),
        ],
        compiler_params=_cparams(("parallel", "parallel", "arbitrary")),
        name="nsa_cmp",
    )(q_bf, p0, p1, p0, p1, smap)


def _stack_heads(q):
    return jnp.concatenate([q[:, r * LANES:(r + 1) * LANES] for r in range(NSA_GROUP)], axis=0)


def _masked_softmax_pv(s, mask, v):
    s = jnp.where(mask, s, NEG_INF)
    p = jnp.where(mask, jnp.exp(s - jnp.max(s, axis=1, keepdims=True)), 0.0)
    p = p / jnp.maximum(jnp.sum(p, axis=1, keepdims=True), TINY)
    return jnp.dot(p.astype(BF16), v, preferred_element_type=F32)


def _combine_branches(gates, o_cmp, o_sel, o_win, tq):
    outs = []
    for r in range(NSA_GROUP):
        rows = slice(r * tq, (r + 1) * tq)
        outs.append(gates[:, r:r + 1] * o_cmp[:, r * LANES:(r + 1) * LANES]
                    + gates[:, NSA_GROUP + r:NSA_GROUP + r + 1] * o_sel[rows]
                    + gates[:, 2 * NSA_GROUP + r:2 * NSA_GROUP + r + 1] * o_win[rows])
    return jnp.concatenate(outs, axis=1)


def _nsa_main_kernel(*refs, tq, tk, n_wblk):
    q_ref = refs[0]
    wk_refs = refs[1:1 + n_wblk]
    wv_refs = refs[1 + n_wblk:1 + 2 * n_wblk]
    sk_ref, svt_ref, sel_ref, ocmp_ref, gate_ref, o_ref, m_ref, l_ref, acc_ref = refs[1 + 2 * n_wblk:]
    qi = pl.program_id(1)
    qs = _stack_heads(q_ref[...])
    n_sel_pad = sel_ref.shape[0]

    wk = jnp.concatenate([r[...] for r in wk_refs], axis=0)
    wv = jnp.concatenate([r[...] for r in wv_refs], axis=0)
    wlen = n_wblk * tq
    qpos_w = qi * tq + lax.broadcasted_iota(jnp.int32, (tq, wlen), 0)
    wpos = (qi - (n_wblk - 1)) * tq + lax.broadcasted_iota(jnp.int32, (tq, wlen), 1)
    diff = qpos_w - wpos
    wm1 = jnp.logical_and(jnp.logical_and(diff >= 0, diff <= WINDOW), wpos >= 0)
    s_w = lax.dot_general(qs, wk, (((1,), (1,)), ((), ())), preferred_element_type=F32) * SCALE
    o_win = _masked_softmax_pv(s_w, jnp.concatenate([wm1] * NSA_GROUP, axis=0), wv)

    m_ref[...] = jnp.full_like(m_ref, NEG_INF)
    l_ref[...] = jnp.zeros_like(l_ref)
    acc_ref[...] = jnp.zeros_like(acc_ref)
    selt = sel_ref[...]
    blocks_per_tile = tk // SEL_BLOCK
    rel = (lax.broadcasted_iota(jnp.int32, (tk, n_sel_pad), 1)
           - lax.broadcasted_iota(jnp.int32, (tk, n_sel_pad), 0) // SEL_BLOCK)
    qpos = qi * tq + lax.broadcasted_iota(jnp.int32, (tk, tq), 1)
    krow = lax.broadcasted_iota(jnp.int32, (tk, tq), 0)

    def body(j, carry):
        start = pl.multiple_of(j * tk, tk)
        expand = jnp.where(rel == j * blocks_per_tile, 1.0, 0.0).astype(BF16)
        picked = jnp.dot(expand, selt, preferred_element_type=F32)
        m1 = jnp.logical_and(picked > 0.5, start + krow <= qpos)
        mask = jnp.concatenate([m1] * NSA_GROUP, axis=1)
        k = sk_ref[pl.ds(start, tk), :]
        s = lax.dot_general(k, qs, (((1,), (1,)), ((), ())), preferred_element_type=F32) * SCALE
        s = jnp.where(mask, s, NEG_INF)
        m_old = m_ref[...]
        m_new = jnp.maximum(m_old, jnp.max(s, axis=0, keepdims=True))
        p = jnp.where(mask, jnp.exp(s - m_new), 0.0)
        alpha = jnp.exp(m_old - m_new)
        l_ref[...] = alpha * l_ref[...] + jnp.sum(p, axis=0, keepdims=True)
        vt = svt_ref[:, pl.ds(start, tk)]
        acc_ref[...] = alpha * acc_ref[...] + jnp.dot(vt, p.astype(BF16), preferred_element_type=F32)
        m_ref[...] = m_new
        return carry

    n_tiles = ((qi + 1) * tq + tk - 1) // tk
    lax.fori_loop(0, n_tiles, body, 0)
    o_sel = (acc_ref[...] / jnp.maximum(l_ref[...], TINY)).T
    o_ref[...] = _combine_branches(gate_ref[...], ocmp_ref[...], o_sel, o_win, tq).astype(o_ref.dtype)


def _nsa_main_prompt(q_rot_bf, win_bf, rows_bf, sv_t, sel_t, o_cmp, gates, *, tq=128, tk=512):
    t = q_rot_bf.shape[0]
    tq = _pick_tile(t, tq)
    tk = _pick_tile(t, tk)
    assert WINDOW % tq == 0 and tk % SEL_BLOCK == 0
    n_wblk = WINDOW // tq + 1
    n_sel_pad = sel_t.shape[1]
    gw = NSA_GROUP * HEAD_DIM

    def win_spec(s, col0):
        return pl.BlockSpec((tq, HEAD_DIM), lambda g, i: (jnp.maximum(i - (n_wblk - 1) + s, 0), col0 + g))

    in_specs = [pl.BlockSpec((tq, gw), lambda g, i: (i, g))]
    in_specs += [win_spec(s, 0) for s in range(n_wblk)]
    in_specs += [win_spec(s, NSA_KV_HEADS) for s in range(n_wblk)]
    in_specs += [
        pl.BlockSpec((t, HEAD_DIM), lambda g, i: (0, 2 * NSA_KV_HEADS + g)),
        pl.BlockSpec((None, HEAD_DIM, t), lambda g, i: (g, 0, 0)),
        pl.BlockSpec((None, n_sel_pad, tq), lambda g, i: (g, 0, i)),
        pl.BlockSpec((tq, gw), lambda g, i: (i, g)),
        pl.BlockSpec((tq, LANES), lambda g, i: (i, g)),
    ]
    return pl.pallas_call(
        functools.partial(_nsa_main_kernel, tq=tq, tk=tk, n_wblk=n_wblk),
        grid=(NSA_KV_HEADS, t // tq),
        in_specs=in_specs,
        out_specs=pl.BlockSpec((tq, gw), lambda g, i: (i, g)),
        out_shape=jax.ShapeDtypeStruct((t, NSA_Q_HEADS * HEAD_DIM), BF16),
        scratch_shapes=[
            pltpu.VMEM((1, NSA_GROUP * tq), F32),
            pltpu.VMEM((1, NSA_GROUP * tq), F32),
            pltpu.VMEM((HEAD_DIM, NSA_GROUP * tq), F32),
        ],
        compiler_params=_cparams(("parallel", "arbitrary")),
        name="nsa_main_prompt",
    )(q_rot_bf, *([win_bf] * (2 * n_wblk)), rows_bf, sv_t, sel_t, o_cmp, gates)


ROWS_PAD = 16


def _sb_sample_kernel(pt_ref, q_ref, new_ref, cache_ref, u_ref, o_ref, buf_ref, sem, acc_ref, carry_ref,
                      *, n_pages, past, n_new):
    bi = pl.program_id(0)
    qpos = past + n_new - 1
    row = lax.broadcasted_iota(jnp.int32, (ROWS_PAD, HEAD_DIM), 0)
    lane = lax.broadcasted_iota(jnp.int32, (ROWS_PAD, PAGE_SIZE), 1)
    prow = lax.broadcasted_iota(jnp.int32, (PAGE_SIZE, HEAD_DIM), 0)
    q16 = jnp.concatenate([q_ref[...], jnp.zeros((ROWS_PAD - SB_HEADS, HEAD_DIM), F32)], axis=0)
    q_rows = [jnp.where(row == h, q16, 0.0).astype(BF16) for h in range(SB_HEADS)]
    acc_ref[...] = jnp.zeros_like(acc_ref)
    carry_ref[...] = jnp.zeros_like(carry_ref)
    u = u_ref[...]

    def page_copy(j, slot):
        return pltpu.make_async_copy(cache_ref.at[pt_ref[bi, n_pages - 1 - j]], buf_ref.at[slot], sem.at[slot])

    def process(get_k, get_v, kpos0, n_valid):
        z = None
        for h in range(SB_HEADS):
            zh = lax.dot_general(q_rows[h], get_k(h).astype(BF16), (((1,), (1,)), ((), ())),
                                 preferred_element_type=F32)
            z = zh if z is None else z + zh
        mask = jnp.logical_and(kpos0 + lane < qpos, lane < n_valid)
        a, carry = _sb_tile(z * SCALE, u, carry_ref[...], mask)
        for h in range(SB_HEADS):
            acc_ref[h] += jnp.dot(a, get_v(h).astype(BF16), preferred_element_type=F32)
        carry_ref[...] = carry
        return (jnp.max(carry[:SB_HEADS]) < SB_DEAD_LOG).astype(jnp.int32)

    def new_rows(r):
        return jnp.where(prow < n_new, jnp.broadcast_to(new_ref[r:r + 1, :], (PAGE_SIZE, HEAD_DIM)), 0.0)

    page_copy(0, 0).start()
    done0 = process(new_rows, lambda h: new_rows(SB_HEADS + h), past, n_new)

    def cond(state):
        j, done = state
        return jnp.logical_and(j < n_pages, done == 0)

    def body(state):
        j, _ = state
        slot = j % 2
        page_copy(j, slot).wait()

        @pl.when(j + 1 < n_pages)
        def _():
            page_copy(j + 1, 1 - slot).start()

        done = process(lambda h: buf_ref[slot, pl.ds(h, PAGE_SIZE, stride=SB_CACHE_SLOTS), :],
                       lambda h: buf_ref[slot, pl.ds(SB_HEADS + h, PAGE_SIZE, stride=SB_CACHE_SLOTS), :],
                       (n_pages - 1 - j) * PAGE_SIZE, PAGE_SIZE)
        return j + 1, done

    j_end, _ = lax.while_loop(cond, body, (jnp.int32(0), done0))

    @pl.when(j_end < n_pages)
    def _():
        page_copy(j_end, j_end % 2).wait()

    row8 = lax.broadcasted_iota(jnp.int32, (SB_HEADS, HEAD_DIM), 0)
    out = jnp.zeros((SB_HEADS, HEAD_DIM), F32)
    for h in range(SB_HEADS):
        out = jnp.where(row8 == h, acc_ref[h][:SB_HEADS], out)
    o_ref[...] = out


def _sb_sample(q, kv_new, cache, page_table):
    b, n_pages = page_table.shape
    page_rows = PAGE_SIZE * SB_CACHE_SLOTS
    grid_spec = pltpu.PrefetchScalarGridSpec(
        num_scalar_prefetch=1,
        grid=(b,),
        in_specs=[
            pl.BlockSpec((None, SB_HEADS, HEAD_DIM), lambda bi, pt: (bi, 0, 0)),
            pl.BlockSpec((None, SB_CACHE_SLOTS, HEAD_DIM), lambda bi, pt: (bi, 0, 0)),
            pl.BlockSpec(memory_space=pl.ANY),
            pl.BlockSpec((PAGE_SIZE, PAGE_SIZE), lambda bi, pt: (0, 0)),
        ],
        out_specs=pl.BlockSpec((None, SB_HEADS, HEAD_DIM), lambda bi, pt: (bi, 0, 0)),
        scratch_shapes=[
            pltpu.VMEM((2, page_rows, HEAD_DIM), F32),
            pltpu.SemaphoreType.DMA((2,)),
            pltpu.VMEM((SB_HEADS, ROWS_PAD, HEAD_DIM), F32),
            pltpu.VMEM((ROWS_PAD, 1), F32),
        ],
    )
    return pl.pallas_call(
        functools.partial(_sb_sample_kernel, n_pages=n_pages, past=n_pages * PAGE_SIZE, n_new=1),
        grid_spec=grid_spec,
        out_shape=jax.ShapeDtypeStruct((b, SB_HEADS, HEAD_DIM), F32),
        compiler_params=_cparams(("arbitrary",)),
        name="sb_sample",
    )(page_table, q, kv_new, cache, _later_matrix(PAGE_SIZE))


def _nsa_main_sample_kernel(*refs, n_top, past, n_new):
    kh = NSA_KV_HEADS
    idx_ref, pt_ref, q_ref = refs[:3]
    blk_refs = refs[3:3 + kh * n_top]
    new_ref, win_ref, wnew_ref, ocmp_ref, gate_ref, o_ref = refs[3 + kh * n_top:]
    bi = pl.program_id(0)
    qpos = past + n_new - 1
    win_buf = win_ref.shape[0] // WIN_CACHE_SLOTS
    n_keys = n_top * SEL_BLOCK
    lane = lax.broadcasted_iota(jnp.int32, (ROWS_PAD, n_keys), 1)
    seg = lane // SEL_BLOCK

    def new_tile(ref, r, rows):
        row = lax.broadcasted_iota(jnp.int32, (rows, HEAD_DIM), 0)
        return jnp.where(row < n_new, jnp.broadcast_to(ref[r:r + 1, :], (rows, HEAD_DIM)), 0.0)

    def attend(q, k, v, mask):
        s = lax.dot_general(q, k.astype(BF16), (((1,), (1,)), ((), ())), preferred_element_type=F32) * SCALE
        return _masked_softmax_pv(s, mask, v.astype(BF16))

    for g in range(kh):
        q = q_ref[g]
        ks, vs = [], []
        base = jnp.zeros((ROWS_PAD, n_keys), jnp.int32)
        found = jnp.zeros((ROWS_PAD, n_keys), jnp.int32)
        for n in range(n_top):
            blk = idx_ref[(bi * kh + g) * n_top + n]
            is_new = blk * SEL_BLOCK >= past
            ref = blk_refs[g * n_top + n]
            kc = ref[pl.ds(2 * kh + g, SEL_BLOCK, stride=NSA_CACHE_SLOTS), :]
            vc = ref[pl.ds(3 * kh + g, SEL_BLOCK, stride=NSA_CACHE_SLOTS), :]
            ks.append(jnp.where(is_new, new_tile(new_ref, 2 * kh + g, SEL_BLOCK), kc))
            vs.append(jnp.where(is_new, new_tile(new_ref, 3 * kh + g, SEL_BLOCK), vc))
            base = jnp.where(seg == n, blk * SEL_BLOCK, base)
            found = jnp.where(seg == n, (blk >= 0).astype(jnp.int32), found)
        kpos = base + lane % SEL_BLOCK
        mask = jnp.logical_and(found > 0, jnp.logical_and(kpos <= qpos, kpos < past + n_new))
        o_sel = attend(q, jnp.concatenate(ks, axis=0), jnp.concatenate(vs, axis=0), mask)

        wk = jnp.concatenate([win_ref[pl.ds(g, win_buf, stride=WIN_CACHE_SLOTS), :],
                              new_tile(wnew_ref, g, LANES)], axis=0)
        wv = jnp.concatenate([win_ref[pl.ds(kh + g, win_buf, stride=WIN_CACHE_SLOTS), :],
                              new_tile(wnew_ref, kh + g, LANES)], axis=0)
        wlane = lax.broadcasted_iota(jnp.int32, (ROWS_PAD, win_buf + LANES), 1)
        wpos = past - win_buf + wlane
        diff = qpos - wpos
        wmask = jnp.logical_and(jnp.logical_and(diff >= 0, diff <= WINDOW),
                                jnp.logical_and(wpos >= 0, wpos < past + n_new))
        o_win = attend(q, wk, wv, wmask)
        gates = gate_ref[g]
        o_ref[g] = gates[:, 0:1] * ocmp_ref[g] + gates[:, 1:2] * o_sel + gates[:, 2:3] * o_win


def _nsa_main_sample(idx, page_table, q_rot, cache, rows_new, win_state, win_new, o_cmp, gates, *, n_top):
    b, n_pages = page_table.shape
    past = n_pages * PAGE_SIZE
    per_page = PAGE_SIZE // SEL_BLOCK
    kh = NSA_KV_HEADS

    def sel_spec(g, n):
        def index(bi, idx_ref, pt):
            blk = jnp.maximum(idx_ref[(bi * kh + g) * n_top + n], 0)
            page = jnp.minimum(blk // per_page, n_pages - 1)
            return pt[bi, page], blk % per_page, 0
        return pl.BlockSpec((None, SEL_BLOCK * NSA_CACHE_SLOTS, HEAD_DIM), index)

    def per_seq(rows):
        return pl.BlockSpec((None, rows, HEAD_DIM), lambda bi, idx_ref, pt: (bi, 0, 0))

    head_spec = pl.BlockSpec((None, kh, ROWS_PAD, HEAD_DIM), lambda bi, idx_ref, pt: (bi, 0, 0, 0))
    grid_spec = pltpu.PrefetchScalarGridSpec(
        num_scalar_prefetch=2,
        grid=(b,),
        in_specs=[head_spec] + [sel_spec(g, n) for g in range(kh) for n in range(n_top)] + [
            per_seq(NSA_CACHE_SLOTS), per_seq(win_state.shape[1]), per_seq(WIN_CACHE_SLOTS), head_spec, head_spec],
        out_specs=head_spec,
    )
    return pl.pallas_call(
        functools.partial(_nsa_main_sample_kernel, n_top=n_top, past=past, n_new=1),
        grid_spec=grid_spec,
        out_shape=jax.ShapeDtypeStruct((b, kh, ROWS_PAD, HEAD_DIM), F32),
        compiler_params=_cparams(("parallel",)),
        name="nsa_main_sample",
    )(idx, page_table, q_rot, *([cache] * (kh * n_top)), rows_new, win_state, win_new, o_cmp, gates)


def _merge_kernel(osb_ref, onsa_ref, g0_ref, g1_ref, pa_ref, pb_ref, wo_ref, x_ref, out_ref, acc_ref):
    j = pl.program_id(1)

    @pl.when(j == 0)
    def _():
        acc_ref[...] = jnp.zeros_like(acc_ref)

    a = jnp.dot(osb_ref[...], pa_ref[...], preferred_element_type=F32)
    b = jnp.dot(onsa_ref[...], pb_ref[...], preferred_element_type=F32)
    m = (g0_ref[...] * a + g1_ref[...] * b).astype(BF16)
    acc_ref[...] += jnp.dot(m, wo_ref[...], preferred_element_type=F32)

    @pl.when(j == pl.num_programs(1) - 1)
    def _():
        out_ref[...] = x_ref[...] + acc_ref[...]


def _merge(o_sb, o_nsa, mg, pa, pb, wo, x, *, tm_pref=512, tn=512):
    m, d = x.shape
    ka, kb = o_sb.shape[1], o_nsa.shape[1]
    tm = _pick_tile(m, tm_pref)
    tn = _pick_tile(d, tn)
    nj = d // tn
    return pl.pallas_call(
        _merge_kernel,
        grid=(m // tm, nj),
        in_specs=[
            pl.BlockSpec((tm, ka), lambda i, j: (i, 0)),
            pl.BlockSpec((tm, kb), lambda i, j: (i, 0)),
            pl.BlockSpec((tm, tn), lambda i, j: (i, j)),
            pl.BlockSpec((tm, tn), lambda i, j: (i, nj + j)),
            pl.BlockSpec((ka, tn), lambda i, j: (0, j)),
            pl.BlockSpec((kb, tn), lambda i, j: (0, j)),
            pl.BlockSpec((tn, d), lambda i, j: (j, 0)),
            pl.BlockSpec((tm, d), lambda i, j: (i, 0)),
        ],
        out_specs=pl.BlockSpec((tm, d), lambda i, j: (i, 0)),
        out_shape=jax.ShapeDtypeStruct((m, d), F32),
        scratch_shapes=[pltpu.VMEM((tm, d), F32)],
        compiler_params=_cparams(("parallel", "arbitrary")),
        name="merge",
    )(o_sb, o_nsa, mg, mg, pa, pb, wo, x)


SB_Q_COLS = SB_HEADS * HEAD_DIM
SB_KV_COLS = 2 * SB_HEADS * HEAD_DIM
NSA_Q_COLS = NSA_Q_HEADS * HEAD_DIM
NSA_ROW_COLS = 4 * NSA_KV_HEADS * HEAD_DIM
NSA_WIN_COLS = 2 * NSA_KV_HEADS * HEAD_DIM
NSA_GATE_COLS = 3 * NSA_Q_HEADS


def _prep_layer_weights(layer, g_ffn1, ffn1_w_gu, ffn1_w_down, g_mix, w_in, cmp_pe_k, cmp_w_k, cmp_pe_v, cmp_w_v,
                        p_a, p_b, w_o, g_ffn2, ffn2_w_gu, ffn2_w_down):
    w = {}
    w["g1"], w["g_mix"], w["g2"] = g_ffn1[layer][None], g_mix[layer][None], g_ffn2[layer][None]
    w["ffn1"] = _prep_ffn_weights(ffn1_w_gu[layer], ffn1_w_down[layer])
    w["ffn2"] = _prep_ffn_weights(ffn2_w_gu[layer], ffn2_w_down[layer])
    wi = w_in[layer]
    off = 0
    for name, n in (("sbq", SB_Q_COLS), ("sbkv", SB_KV_COLS), ("nq", NSA_Q_COLS), ("rows", NSA_ROW_COLS),
                    ("win", NSA_WIN_COLS)):
        w[name] = wi[:, off:off + n].astype(BF16)
        off += n
    wg = wi[:, off:off + NSA_GATE_COLS].reshape(-1, 3, NSA_KV_HEADS, NSA_GROUP).transpose(0, 2, 1, 3)
    wg = wg.reshape(-1, NSA_KV_HEADS, 3 * NSA_GROUP)
    wg = jnp.pad(wg, ((0, 0), (0, 0), (0, LANES - 3 * NSA_GROUP)))
    w["gate"] = wg.reshape(-1, NSA_KV_HEADS * LANES).astype(BF16)
    off += NSA_GATE_COLS
    w["mg"] = wi[:, off:].astype(BF16)
    w["pe"] = jnp.stack([cmp_pe_k[layer], cmp_pe_v[layer]])
    w["wc"] = jnp.stack([cmp_w_k[layer], cmp_w_v[layer]]).astype(BF16)
    w["pa"], w["pb"], w["wo"] = p_a[layer].astype(BF16), p_b[layer].astype(BF16), w_o[layer].astype(BF16)
    return w


def _mixer_inputs(h, w, cos2, sin2):
    plain_bf = ("plain", None, BF16)
    (sbq,) = _proj(h, w["sbq"], (plain_bf,))
    kv_f, kv_b = _proj(h, w["sbkv"], (("plain", None, F32), plain_bf))
    nq_raw, nq_rot = _proj(h, w["nq"], (plain_bf, ("rope", None, BF16)), cos2, sin2)
    rows_f, rows_b = _proj(h, w["rows"], (("rope", (2,), F32), ("rope", (2,), BF16)), cos2, sin2)
    win_f, win_b = _proj(h, w["win"], (("rope", (0,), F32), ("rope", (0,), BF16)), cos2, sin2)
    (gates,) = _proj(h, w["gate"], (("sigmoid", None, F32),), tn=LANES)
    (mg,) = _proj(h, w["mg"], (("sigmoid", None, F32),))
    return dict(sbq=sbq, kv_f=kv_f, kv_b=kv_b, nq_raw=nq_raw, nq_rot=nq_rot, rows_f=rows_f, rows_b=rows_b,
                win_f=win_f, win_b=win_b, gates=gates, mg=mg)


def _prompt_layer(x, w, cos2, sin2, g_next, last):
    t = x.shape[0]
    assert t % LANES == 0 and t >= CMP_BLOCK
    x1, h = _ffn(x, w["g1"], *w["ffn1"], w["g_mix"], emit_x=True, norm_dtype=BF16)
    mi = _mixer_inputs(h, w, cos2, sin2)
    o_sb = _sb_prompt(mi["sbq"], mi["kv_b"])
    p0, p1 = _compress_prompt(mi["rows_f"], w["pe"], w["wc"])
    n_cmp = (t - CMP_BLOCK) // CMP_STRIDE + 1
    n_sel = -(-t // SEL_BLOCK)
    tq = _pick_tile(t, 128)
    o_cmp, sel_t, _ = _nsa_cmp(mi["nq_raw"][None], p0, p1, tq=tq, qpos0=0, n_cmp=n_cmp, n_sel=n_sel, sel_t=True)
    sv_cols = mi["rows_b"][:, 3 * NSA_KV_HEADS * HEAD_DIM:]
    sv_t = sv_cols.reshape(t, NSA_KV_HEADS, HEAD_DIM).transpose(1, 2, 0)
    o_nsa = _nsa_main_prompt(mi["nq_rot"], mi["win_b"], mi["rows_b"], sv_t, sel_t[0], o_cmp[0], mi["gates"])
    x2 = _merge(o_sb, o_nsa, mi["mg"], w["pa"], w["pb"], w["wo"], x1)
    if last:
        (y,) = _ffn(x2, w["g2"], *w["ffn2"], g_next, emit_x=False, norm_dtype=F32)
        x3 = None
    else:
        x3, y = _ffn(x2, w["g2"], *w["ffn2"], g_next, emit_x=True, norm_dtype=F32)
    return x3, y, mi["kv_f"], mi["rows_f"], mi["win_f"]


def _pad_rows(a, rows):
    return jnp.pad(a, ((0, 0), (0, 0), (0, rows - a.shape[2]), (0, 0)))


def _sample_layer(x, w, cos2, sin2, cache_sb, cache_nsa, win_state, page_table, g_next, last):
    b = x.shape[0]
    n_pages = page_table.shape[1]
    past = n_pages * PAGE_SIZE
    total = past + 1
    kh, grp = NSA_KV_HEADS, NSA_GROUP
    x1, h = _ffn(x, w["g1"], *w["ffn1"], w["g_mix"], emit_x=True, norm_dtype=BF16)
    mi = _mixer_inputs(h, w, cos2, sin2)
    o_sb = _sb_sample(mi["sbq"].astype(F32).reshape(b, SB_HEADS, HEAD_DIM),
                      mi["kv_f"].reshape(b, SB_CACHE_SLOTS, HEAD_DIM), cache_sb, page_table)
    o_sb = o_sb.reshape(b, SB_HEADS * HEAD_DIM)

    p0, p1 = _compress_sample(cache_nsa, page_table, w["pe"], w["wc"])
    n_cmp = (total - CMP_BLOCK) // CMP_STRIDE + 1
    n_sel = -(-total // SEL_BLOCK)
    n_top = min(SEL_TOPN, n_sel)
    q_raw = jnp.pad(mi["nq_raw"][:, None, :], ((0, 0), (0, ROWS_PAD - 1), (0, 0)))
    o_cmp, _, idx = _nsa_cmp(q_raw, p0, p1, tq=ROWS_PAD, qpos0=past, n_cmp=n_cmp, n_sel=n_sel, sel_t=False)
    idx = idx[:, :, 0, :n_top].reshape(-1)
    q_rot = _pad_rows(mi["nq_rot"].reshape(b, kh, grp, HEAD_DIM), ROWS_PAD)
    o_cmp = _pad_rows(o_cmp[:, 0, :].reshape(b, kh, grp, HEAD_DIM), ROWS_PAD)
    gates = mi["gates"].reshape(b, kh, LANES)[:, :, :3 * grp].reshape(b, kh, 3, grp).transpose(0, 1, 3, 2)
    gates = jnp.pad(gates, ((0, 0), (0, 0), (0, ROWS_PAD - grp), (0, LANES - 3)))
    o_nsa = _nsa_main_sample(idx, page_table, q_rot, cache_nsa, mi["rows_f"].reshape(b, NSA_CACHE_SLOTS, HEAD_DIM),
                             win_state, mi["win_f"].reshape(b, WIN_CACHE_SLOTS, HEAD_DIM), o_cmp, gates, n_top=n_top)
    o_nsa = o_nsa[:, :, :grp, :].reshape(b, NSA_Q_HEADS * HEAD_DIM)

    x2 = _merge(o_sb.astype(BF16), o_nsa.astype(BF16), mi["mg"], w["pa"], w["pb"], w["wo"], x1)
    if last:
        (y,) = _ffn(x2, w["g2"], *w["ffn2"], g_next, emit_x=False, norm_dtype=F32)
        x3 = None
    else:
        x3, y = _ffn(x2, w["g2"], *w["ffn2"], g_next, emit_x=True, norm_dtype=F32)
    return x3, y, mi["kv_f"], mi["rows_f"], mi["win_f"]


def kernel(x_prompt, x_sample, cache_sb_kv, cache_nsa_kv, state_win_kv, page_table, g_ffn1, ffn1_w_gu, ffn1_w_down,
           g_mix, w_in, cmp_pe_k, cmp_w_k, cmp_pe_v, cmp_w_v, p_a, p_b, w_o, g_ffn2, ffn2_w_gu, ffn2_w_down, g_final):
    bp, t, d = x_prompt.shape
    db, ds, _ = x_sample.shape
    depth = g_ffn1.shape[0]
    n_pool = cache_sb_kv.shape[1]
    assert bp == 1 and ds == 1
    past = page_table.shape[1] * PAGE_SIZE
    win_buf = state_win_kv.shape[2]
    cos_p, sin_p = _rope_tables(jnp.arange(t))
    cos_s, sin_s = _rope_tables(jnp.full((db,), past, jnp.int32))
    g_fin = g_final[None]

    xp = x_prompt.reshape(t, d)
    xs = x_sample.reshape(db, d)
    sb_p, nsa_p, win_p, sb_s, nsa_s, win_s = [], [], [], [], [], []
    yp = ys = None
    for layer in range(depth):
        w = _prep_layer_weights(layer, g_ffn1, ffn1_w_gu, ffn1_w_down, g_mix, w_in, cmp_pe_k, cmp_w_k, cmp_pe_v,
                                cmp_w_v, p_a, p_b, w_o, g_ffn2, ffn2_w_gu, ffn2_w_down)
        last = layer == depth - 1
        xp, yp, kv_f, rows_f, win_f = _prompt_layer(xp, w, cos_p, sin_p, g_fin, last)
        sb_p.append(kv_f.reshape(1, t, 2, SB_HEADS, HEAD_DIM))
        nsa_p.append(rows_f.reshape(1, t, 4, NSA_KV_HEADS, HEAD_DIM))
        wp = min(WINDOW, t)
        win_p.append(win_f[t - wp:].reshape(1, wp, 2, NSA_KV_HEADS, HEAD_DIM))

        cache_sb = cache_sb_kv[layer].reshape(n_pool, PAGE_SIZE * SB_CACHE_SLOTS, HEAD_DIM)
        cache_nsa = cache_nsa_kv[layer].reshape(n_pool, PAGE_SIZE * NSA_CACHE_SLOTS, HEAD_DIM)
        win_state = state_win_kv[layer].reshape(db, win_buf * WIN_CACHE_SLOTS, HEAD_DIM)
        xs, ys, kv_f, rows_f, win_f = _sample_layer(xs, w, cos_s, sin_s, cache_sb, cache_nsa, win_state, page_table,
                                                    g_fin, last)
        sb_s.append(kv_f.reshape(db, 1, 2, SB_HEADS, HEAD_DIM))
        nsa_s.append(rows_f.reshape(db, 1, 4, NSA_KV_HEADS, HEAD_DIM))
        win_all = jnp.concatenate([state_win_kv[layer], win_f.reshape(db, 1, 2, NSA_KV_HEADS, HEAD_DIM)], axis=1)
        win_s.append(win_all[:, win_all.shape[1] - win_buf:])
    return (yp.reshape(1, t, d), ys.reshape(db, 1, d), jnp.stack(sb_p), jnp.stack(nsa_p), jnp.stack(win_p),
            jnp.stack(sb_s), jnp.stack(nsa_s), jnp.stack(win_s))
```

```python
import functools

import numpy as np
import jax
import jax.numpy as jnp
from jax import lax
from jax.experimental import pallas as pl
from jax.experimental.pallas import tpu as pltpu

HEAD_DIM = 128
SB_HEADS = 8
NSA_Q_HEADS = 8
NSA_KV_HEADS = 2
NSA_GROUP = NSA_Q_HEADS // NSA_KV_HEADS
CMP_BLOCK = 32
CMP_STRIDE = 16
SEL_BLOCK = 64
SEL_TOPN = 16
WINDOW = 512
PAGE_SIZE = 128
ROPE_THETA = 10000.0
NORM_EPS = 1e-6
HALF_STEP = 0.5
SCALE = HEAD_DIM ** -0.5
NEG_INF = -1e30
FORCED_SCORE = 1e4
TINY = 1e-30

SB_DEAD_LOG = -104.0

EXP2_SCALE = SCALE * float(np.log2(np.e))
MAX_FLOOR = 0.1 * NEG_INF

LANES = 128
SB_CACHE_SLOTS = 2 * SB_HEADS
NSA_CACHE_SLOTS = 4 * NSA_KV_HEADS
WIN_CACHE_SLOTS = 2 * NSA_KV_HEADS
VMEM_LIMIT = 56 * 1024 * 1024

F32 = jnp.float32
BF16 = jnp.bfloat16


def _cparams(sem):
    return pltpu.CompilerParams(dimension_semantics=sem, vmem_limit_bytes=VMEM_LIMIT)


def _round_up(x, m):
    return -(-x // m) * m


def _pick_tile(n, pref):
    t = min(pref, n)
    while n % t:
        t //= 2
    return t


def _rms(x, g):
    return x * lax.rsqrt(jnp.mean(x * x, axis=-1, keepdims=True) + NORM_EPS) * g


def _sigmoid(x):
    return 1.0 / (1.0 + jnp.exp(-x))


def _ffn_kernel(x_ref, g_ref, wg_ref, wu_ref, wd_ref, g2_ref, *refs, emit_x):
    if emit_x:
        out_ref, n_ref, h_scr, acc_scr = refs
    else:
        n_ref, h_scr, acc_scr = refs
    j = pl.program_id(1)

    @pl.when(j == 0)
    def _():
        h_scr[...] = _rms(x_ref[...], g_ref[...]).astype(BF16)
        acc_scr[...] = jnp.zeros_like(acc_scr)

    h = h_scr[...]
    gate = jnp.dot(h, wg_ref[...], preferred_element_type=F32)
    up = jnp.dot(h, wu_ref[...], preferred_element_type=F32)
    act = (gate * _sigmoid(gate) * up).astype(BF16)
    acc_scr[...] += jnp.dot(act, wd_ref[...], preferred_element_type=F32)

    @pl.when(j == pl.num_programs(1) - 1)
    def _():
        y = x_ref[...] + HALF_STEP * acc_scr[...]
        if emit_x:
            out_ref[...] = y
        n_ref[...] = _rms(y, g2_ref[...]).astype(n_ref.dtype)


def _ffn(x, g, wg, wu, wd, g2, *, emit_x, norm_dtype, tm_pref=512, tf=512):
    m, d = x.shape
    fpad = wg.shape[1]
    tm = _pick_tile(m, tm_pref)
    grid = (m // tm, fpad // tf)
    out_shape = []
    out_specs = []
    if emit_x:
        out_shape.append(jax.ShapeDtypeStruct((m, d), F32))
        out_specs.append(pl.BlockSpec((tm, d), lambda i, j: (i, 0)))
    out_shape.append(jax.ShapeDtypeStruct((m, d), norm_dtype))
    out_specs.append(pl.BlockSpec((tm, d), lambda i, j: (i, 0)))
    return pl.pallas_call(
        functools.partial(_ffn_kernel, emit_x=emit_x),
        grid=grid,
        in_specs=[
            pl.BlockSpec((tm, d), lambda i, j: (i, 0)),
            pl.BlockSpec((1, d), lambda i, j: (0, 0)),
            pl.BlockSpec((d, tf), lambda i, j: (0, j)),
            pl.BlockSpec((d, tf), lambda i, j: (0, j)),
            pl.BlockSpec((tf, d), lambda i, j: (j, 0)),
            pl.BlockSpec((1, d), lambda i, j: (0, 0)),
        ],
        out_specs=out_specs,
        out_shape=out_shape,
        scratch_shapes=[pltpu.VMEM((tm, d), BF16), pltpu.VMEM((tm, d), F32)],
        compiler_params=_cparams(("parallel", "arbitrary")),
        name="ffn",
    )(x, g, wg, wu, wd, g2)


def _prep_ffn_weights(w_gu, w_down, tf=512):
    d, two_f = w_gu.shape
    f = two_f // 2
    fpad = _round_up(f, tf)
    wg = jnp.pad(w_gu[:, :f].astype(BF16), ((0, 0), (0, fpad - f)))
    wu = jnp.pad(w_gu[:, f:].astype(BF16), ((0, 0), (0, fpad - f)))
    wd = jnp.pad(w_down.astype(BF16), ((0, fpad - f), (0, 0)))
    return wg, wu, wd


def _rope_tile(x, cos2, sin2):
    return x * cos2 + pltpu.roll(x, HEAD_DIM // 2, axis=1) * sin2


def _proj_kernel(a_ref, w_ref, *refs, outs, use_rope):
    if use_rope:
        cos_ref, sin_ref = refs[:2]
        refs = refs[2:]
    j = pl.program_id(1)
    acc = jnp.dot(a_ref[...], w_ref[...], preferred_element_type=F32)
    tn = acc.shape[1]
    for (mode, rope_tiles, _), o_ref in zip(outs, refs):
        if mode == "sigmoid":
            o_ref[...] = _sigmoid(acc).astype(o_ref.dtype)
        elif mode == "rope":
            cos2 = cos_ref[...]
            sin2 = sin_ref[...]
            rot = jnp.concatenate(
                [_rope_tile(acc[:, c * LANES:(c + 1) * LANES], cos2, sin2) for c in range(tn // LANES)], axis=1)
            if rope_tiles is None:
                o_ref[...] = rot.astype(o_ref.dtype)
            else:
                is_rope = functools.reduce(jnp.logical_or, [j == t for t in rope_tiles])
                o_ref[...] = jnp.where(is_rope, rot, acc).astype(o_ref.dtype)
        else:
            o_ref[...] = acc.astype(o_ref.dtype)


def _proj(a, w, outs, cos2=None, sin2=None, *, tm_pref=1024, tn=256):
    m, k = a.shape
    n = w.shape[1]
    tm = _pick_tile(m, tm_pref)
    tn = min(tn, n)
    use_rope = any(mode == "rope" for mode, _, _ in outs)
    in_specs = [pl.BlockSpec((tm, k), lambda i, j: (i, 0)), pl.BlockSpec((k, tn), lambda i, j: (0, j))]
    args = [a, w]
    if use_rope:
        in_specs += [pl.BlockSpec((tm, LANES), lambda i, j: (i, 0))] * 2
        args += [cos2, sin2]
    res = pl.pallas_call(
        functools.partial(_proj_kernel, outs=outs, use_rope=use_rope),
        grid=(m // tm, n // tn),
        in_specs=in_specs,
        out_specs=[pl.BlockSpec((tm, tn), lambda i, j: (i, j)) for _ in outs],
        out_shape=[jax.ShapeDtypeStruct((m, n), dt) for _, _, dt in outs],
        compiler_params=_cparams(("parallel", "arbitrary")),
        name="proj",
    )(*args)
    return res


def _rope_tables(pos):
    half = HEAD_DIM // 2
    inv_freq = ROPE_THETA ** (-2.0 * jnp.arange(half, dtype=F32) / HEAD_DIM)
    ang = pos.astype(F32)[:, None] * inv_freq[None, :]
    cos, sin = jnp.cos(ang), jnp.sin(ang)
    return jnp.concatenate([cos, cos], axis=1), jnp.concatenate([-sin, sin], axis=1)


def _later_matrix(tk):
    j = np.arange(tk)[:, None]
    s = np.arange(tk)[None, :]
    return jnp.asarray((j > s).astype(np.float32), dtype=BF16)


def _sb_tile(z, u, carry, mask):
    soft = jnp.log(1.0 + jnp.exp(-jnp.abs(z)))
    log_beta = jnp.minimum(z, 0.0) - soft
    log_keep = jnp.where(mask, jnp.minimum(-z, 0.0) - soft, 0.0)
    hi = log_keep.astype(BF16)
    lo = (log_keep - hi.astype(F32)).astype(BF16)
    later = (jnp.dot(hi, u, preferred_element_type=F32) + jnp.dot(lo, u, preferred_element_type=F32)) + carry
    a = jnp.where(mask, jnp.exp(log_beta + later), 0.0)
    return a.astype(BF16), carry + jnp.sum(log_keep, axis=1, keepdims=True)


def _sb_prompt_kernel(q_ref, k_ref, v_ref, u_ref, o_ref, acc_ref, carry_ref, *, tq, tk):
    qi = pl.program_id(1)
    acc_ref[...] = jnp.zeros_like(acc_ref)
    carry_ref[...] = jnp.zeros_like(carry_ref)
    q = q_ref[...]
    u = u_ref[...]
    qpos = qi * tq + lax.broadcasted_iota(jnp.int32, (tq, tk), 0)
    lane = lax.broadcasted_iota(jnp.int32, (tq, tk), 1)

    def cond(state):
        j, done = state
        return jnp.logical_and(j >= 0, done == 0)

    def body(state):
        j, _ = state
        start = pl.multiple_of(j * tk, tk)
        k = k_ref[pl.ds(start, tk), :]
        v = v_ref[pl.ds(start, tk), :]
        z = lax.dot_general(q, k, (((1,), (1,)), ((), ())), preferred_element_type=F32) * SCALE
        mask = (start + lane) < qpos
        a, carry = _sb_tile(z, u, carry_ref[...], mask)
        acc_ref[...] += jnp.dot(a, v, preferred_element_type=F32)
        carry_ref[...] = carry
        done = (jnp.max(carry) < SB_DEAD_LOG).astype(jnp.int32)
        return j - 1, done

    n_tiles = (qi + 1) * (tq // tk)
    lax.while_loop(cond, body, (n_tiles - 1, jnp.int32(0)))
    o_ref[...] = acc_ref[...].astype(o_ref.dtype)


def _sb_prompt(q_bf, kv_bf, *, tq=256, tk=256):
    t = q_bf.shape[0]
    tq = _pick_tile(t, tq)
    tk = min(tk, tq)
    return pl.pallas_call(
        functools.partial(_sb_prompt_kernel, tq=tq, tk=tk),
        grid=(SB_HEADS, t // tq),
        in_specs=[
            pl.BlockSpec((tq, HEAD_DIM), lambda h, i: (i, h)),
            pl.BlockSpec((t, HEAD_DIM), lambda h, i: (0, h)),
            pl.BlockSpec((t, HEAD_DIM), lambda h, i: (0, SB_HEADS + h)),
            pl.BlockSpec((tk, tk), lambda h, i: (0, 0)),
        ],
        out_specs=pl.BlockSpec((tq, HEAD_DIM), lambda h, i: (i, h)),
        out_shape=jax.ShapeDtypeStruct((t, SB_HEADS * HEAD_DIM), BF16),
        scratch_shapes=[pltpu.VMEM((tq, HEAD_DIM), F32), pltpu.VMEM((tq, 1), F32)],
        compiler_params=_cparams(("parallel", "arbitrary")),
        name="sb_prompt",
    )(q_bf, kv_bf, kv_bf, _later_matrix(tk))


def _compress_kernel(*refs, n_in, rows, prefetch):
    if prefetch:
        refs = refs[1:]
    in_refs = refs[:n_in]
    pe_ref, w_ref, p0_ref, p1_ref, col_ref = refs[n_in:]
    nch = rows // CMP_STRIDE
    for sg in range(2 * NSA_KV_HEADS):
        slot = sg // NSA_KV_HEADS
        for i, r in enumerate(in_refs):
            if prefetch:
                col_ref[i * rows:(i + 1) * rows, :] = r[pl.ds(sg, rows, stride=NSA_CACHE_SLOTS), :]
            else:
                col_ref[i * rows:(i + 1) * rows, :] = r[:, sg * LANES:(sg + 1) * LANES]
        acc0 = jnp.zeros((n_in * nch, HEAD_DIM), F32)
        acc1 = jnp.zeros((n_in * nch, HEAD_DIM), F32)
        for j in range(CMP_STRIDE):
            x = col_ref[pl.ds(j, n_in * nch, stride=CMP_STRIDE), :]
            x0 = (x + pe_ref[slot, pl.ds(j, 1), :]).astype(BF16)
            x1 = (x + pe_ref[slot, pl.ds(CMP_STRIDE + j, 1), :]).astype(BF16)
            acc0 += jnp.dot(x0, w_ref[slot, j], preferred_element_type=F32)
            acc1 += jnp.dot(x1, w_ref[slot, CMP_STRIDE + j], preferred_element_type=F32)
        p0_ref[sg] = acc0
        p1_ref[sg] = acc1


def _compress_prompt(rows_f32, pe, w, *, rows_pref=2048):
    t = rows_f32.shape[0]
    rows = _pick_tile(t, rows_pref)
    nch = rows // CMP_STRIDE
    n_sg = 2 * NSA_KV_HEADS
    out_sds = jax.ShapeDtypeStruct((1, n_sg, t // CMP_STRIDE, HEAD_DIM), F32)
    out_spec = pl.BlockSpec((None, n_sg, nch, HEAD_DIM), lambda i: (0, 0, i, 0))
    return pl.pallas_call(
        functools.partial(_compress_kernel, n_in=1, rows=rows, prefetch=False),
        grid=(t // rows,),
        in_specs=[
            pl.BlockSpec((rows, n_sg * LANES), lambda i: (i, 0)),
            pl.BlockSpec(pe.shape, lambda i: (0, 0, 0)),
            pl.BlockSpec(w.shape, lambda i: (0, 0, 0, 0)),
        ],
        out_specs=[out_spec, out_spec],
        out_shape=[out_sds, out_sds],
        scratch_shapes=[pltpu.VMEM((rows, HEAD_DIM), F32)],
        compiler_params=_cparams(("parallel",)),
        name="compress_prompt",
    )(rows_f32, pe, w)


def _compress_sample(cache, page_table, pe, w, *, pages_per_step=32):
    b, n_pages = page_table.shape
    npp = _pick_tile(n_pages, pages_per_step)
    nch = PAGE_SIZE // CMP_STRIDE
    n_sg = 2 * NSA_KV_HEADS
    out_sds = jax.ShapeDtypeStruct((b, n_sg, n_pages * nch, HEAD_DIM), F32)
    out_spec = pl.BlockSpec((None, n_sg, npp * nch, HEAD_DIM), lambda bi, i, pt: (bi, 0, i, 0))

    def page_spec(k):
        return pl.BlockSpec((None, PAGE_SIZE * NSA_CACHE_SLOTS, HEAD_DIM),
                            lambda bi, i, pt: (pt[bi, i * npp + k], 0, 0))

    grid_spec = pltpu.PrefetchScalarGridSpec(
        num_scalar_prefetch=1,
        grid=(b, n_pages // npp),
        in_specs=[page_spec(k) for k in range(npp)] + [
            pl.BlockSpec(pe.shape, lambda bi, i, pt: (0, 0, 0)),
            pl.BlockSpec(w.shape, lambda bi, i, pt: (0, 0, 0, 0)),
        ],
        out_specs=[out_spec, out_spec],
        scratch_shapes=[pltpu.VMEM((npp * PAGE_SIZE, HEAD_DIM), F32)],
    )
    return pl.pallas_call(
        functools.partial(_compress_kernel, n_in=npp, rows=PAGE_SIZE, prefetch=True),
        grid_spec=grid_spec,
        out_shape=[out_sds, out_sds],
        compiler_params=_cparams(("parallel", "arbitrary")),
        name="compress_sample",
    )(page_table, *([cache] * npp), pe, w)


def _selection_matrix(n_c, n_cmp, n_sel, n_sel_pad):
    ratio = SEL_BLOCK // CMP_STRIDE
    lo = -((CMP_BLOCK - 1) // CMP_STRIDE)
    hi = (SEL_BLOCK - 1) // CMP_STRIDE
    c = np.arange(n_c)[None, :]
    b = np.arange(n_sel_pad)[:, None]
    hit = (c >= ratio * b + lo) & (c <= ratio * b + hi) & (c < n_cmp) & (b < n_sel)
    return jnp.asarray(hit.astype(np.float32), dtype=BF16)


def _split_dot(m01, x):
    hi = x.astype(BF16)
    lo = (x - hi.astype(F32)).astype(BF16)
    return jnp.dot(m01, hi, preferred_element_type=F32) + jnp.dot(m01, lo, preferred_element_type=F32)


def _nsa_cmp_kernel(q_ref, p0k_ref, p1k_ref, p0v_ref, p1v_ref, smap_ref, ocmp_ref, sel_ref, idx_ref, ck_ref, cvt_ref,
                    *, tq, qpos0, n_sel, n_top):
    qi = pl.program_id(2)
    n_c = p0k_ref.shape[0]
    n_sel_pad = smap_ref.shape[0]

    @pl.when(qi == 0)
    def _():
        ck_ref[...] = (p0k_ref[...] + pltpu.roll(p1k_ref[...], n_c - 1, axis=0)).astype(BF16)
        cvt_ref[...] = (p0v_ref[...] + pltpu.roll(p1v_ref[...], n_c - 1, axis=0)).T.astype(BF16)

    q = q_ref[...]
    ck = ck_ref[...]
    cvt = cvt_ref[...]
    qpos_c = qpos0 + qi * tq + lax.broadcasted_iota(jnp.int32, (n_c, tq), 1)
    cend = lax.broadcasted_iota(jnp.int32, (n_c, tq), 0) * CMP_STRIDE + (CMP_BLOCK - 1)
    bias = jnp.where(cend <= qpos_c, 0.0, NEG_INF)
    imp = None
    outs = []
    for r in range(NSA_GROUP):
        s = lax.dot_general(ck, q[:, r * LANES:(r + 1) * LANES], (((1,), (1,)), ((), ())),
                            preferred_element_type=F32) + bias
        m = jnp.maximum(jnp.max(s, axis=0, keepdims=True), MAX_FLOOR)
        p = jnp.exp2((s - m) * EXP2_SCALE)
        p = p * (1.0 / jnp.maximum(jnp.sum(p, axis=0, keepdims=True), TINY))
        outs.append(jnp.dot(cvt, p.astype(BF16), preferred_element_type=F32))
        imp = p if imp is None else imp + p
    o = jnp.concatenate(outs, axis=1).T
    ocmp_ref[...] = jnp.concatenate([o[r * tq:(r + 1) * tq] for r in range(NSA_GROUP)], axis=1)

    imp_sel = _split_dot(smap_ref[...], imp)
    qpos = qpos0 + qi * tq + lax.broadcasted_iota(jnp.int32, (n_sel_pad, tq), 1)
    blk = lax.broadcasted_iota(jnp.int32, (n_sel_pad, tq), 0)
    cur = qpos // SEL_BLOCK
    eligible = jnp.logical_and(blk * SEL_BLOCK <= qpos, blk < n_sel)
    forced = jnp.logical_or(blk == 0, jnp.logical_or(blk == cur, blk == cur - 1))
    score = jnp.where(eligible, jnp.where(forced, FORCED_SCORE, imp_sel), NEG_INF)
    blkf = blk.astype(F32)
    pick = lax.broadcasted_iota(jnp.int32, (idx_ref.shape[0], tq), 0)
    selected = jnp.zeros((n_sel_pad, tq), F32)
    idx_acc = jnp.full(pick.shape, -1.0, F32)
    for i in range(n_top):
        mx = jnp.max(score, axis=0, keepdims=True)
        first = jnp.min(jnp.where(score == mx, blkf, 1e9), axis=0, keepdims=True)
        valid = mx > 0.5 * NEG_INF
        hit = blkf == first
        selected = jnp.where(jnp.logical_and(hit, valid), 1.0, selected)
        idx_acc = jnp.where(pick == i, jnp.where(valid, first, -1.0), idx_acc)
        score = jnp.where(hit, -3e38, score)
    sel_ref[...] = selected.astype(sel_ref.dtype)
    idx_ref[...] = idx_acc.astype(jnp.int32)


def _nsa_cmp(q_bf, p0, p1, *, tq, qpos0, n_cmp, n_sel):
    b, t, _ = q_bf.shape
    n_c = p0.shape[2]
    n_sel_pad = _round_up(n_sel, LANES)
    n_top = min(SEL_TOPN, n_sel)
    pick_rows = _round_up(n_top, 8)
    smap = _selection_matrix(n_c, n_cmp, n_sel, n_sel_pad)
    gw = NSA_GROUP * HEAD_DIM

    def part_spec(slot):
        return pl.BlockSpec((None, None, n_c, HEAD_DIM), lambda bi, g, i: (bi, slot * NSA_KV_HEADS + g, 0, 0))

    return pl.pallas_call(
        functools.partial(_nsa_cmp_kernel, tq=tq, qpos0=qpos0, n_sel=n_sel, n_top=n_top),
        grid=(b, NSA_KV_HEADS, t // tq),
        in_specs=[
            pl.BlockSpec((None, tq, gw), lambda bi, g, i: (bi, i, g)),
            part_spec(0), part_spec(0), part_spec(1), part_spec(1),
            pl.BlockSpec((n_sel_pad, n_c), lambda bi, g, i: (0, 0)),
        ],
        out_specs=[
            pl.BlockSpec((None, tq, gw), lambda bi, g, i: (bi, i, g)),
            pl.BlockSpec((None, None, n_sel_pad, tq), lambda bi, g, i: (bi, g, 0, i)),
            pl.BlockSpec((None, None, pick_rows, tq), lambda bi, g, i: (bi, g, 0, i)),
        ],
        out_shape=[
            jax.ShapeDtypeStruct((b, t, NSA_Q_HEADS * HEAD_DIM), F32),
            jax.ShapeDtypeStruct((b, NSA_KV_HEADS, n_sel_pad, t), BF16),
            jax.ShapeDtypeStruct((b, NSA_KV_HEADS, pick_rows, t), jnp.int32),
        ],
        scratch_shapes=[pltpu.VMEM((n_c, HEAD_DIM), BF16), pltpu.VMEM((HEAD_DIM, n_c), BF16)],
        compiler_params=_cparams(("arbitrary", "arbitrary", "arbitrary")),
        name="nsa_cmp",
    )(q_bf, p0, p1, p0, p1, smap)


def _stack_heads(q):
    return jnp.concatenate([q[:, r * LANES:(r + 1) * LANES] for r in range(NSA_GROUP)], axis=0)


def _masked_softmax_pv(s, mask, v):
    s = jnp.where(mask, s, NEG_INF)
    p = jnp.where(mask, jnp.exp(s - jnp.max(s, axis=1, keepdims=True)), 0.0)
    p = p / jnp.maximum(jnp.sum(p, axis=1, keepdims=True), TINY)
    return jnp.dot(p.astype(BF16), v, preferred_element_type=F32)


def _combine_branches(gates, o_cmp, o_sel, o_win, tq):
    outs = []
    for r in range(NSA_GROUP):
        rows = slice(r * tq, (r + 1) * tq)
        outs.append(gates[:, r:r + 1] * o_cmp[:, r * LANES:(r + 1) * LANES]
                    + gates[:, NSA_GROUP + r:NSA_GROUP + r + 1] * o_sel[rows]
                    + gates[:, 2 * NSA_GROUP + r:2 * NSA_GROUP + r + 1] * o_win[rows])
    return jnp.concatenate(outs, axis=1)


def _nsa_main_kernel(*refs, tq, tk, n_wblk):
    q_ref = refs[0]
    wk_refs = refs[1:1 + n_wblk]
    wv_refs = refs[1 + n_wblk:1 + 2 * n_wblk]
    (sk_ref, svt_ref, sel_ref, ocmp_ref, gate_ref, o_ref, m_ref, l_ref, acc_ref,
     sa_ref, sb_ref, ba_ref, bb_ref) = refs[1 + 2 * n_wblk:]
    qi = pl.program_id(1)
    qs = _stack_heads(q_ref[...])
    n_sel_pad = sel_ref.shape[0]

    wk = jnp.concatenate([r[...] for r in wk_refs], axis=0)
    wv = jnp.concatenate([r[...] for r in wv_refs], axis=0)
    wlen = n_wblk * tq
    qpos_w = qi * tq + lax.broadcasted_iota(jnp.int32, (tq, wlen), 0)
    wpos = (qi - (n_wblk - 1)) * tq + lax.broadcasted_iota(jnp.int32, (tq, wlen), 1)
    diff = qpos_w - wpos
    wm1 = jnp.logical_and(jnp.logical_and(diff >= 0, diff <= WINDOW), wpos >= 0)
    s_w = lax.dot_general(qs, wk, (((1,), (1,)), ((), ())), preferred_element_type=F32) * SCALE
    o_win = _masked_softmax_pv(s_w, jnp.concatenate([wm1] * NSA_GROUP, axis=0), wv)

    m_ref[...] = jnp.full_like(m_ref, MAX_FLOOR)
    l_ref[...] = jnp.zeros_like(l_ref)
    acc_ref[...] = jnp.zeros_like(acc_ref)
    selt = sel_ref[...]
    blocks_per_tile = tk // SEL_BLOCK
    rel = (lax.broadcasted_iota(jnp.int32, (tk, n_sel_pad), 1)
           - lax.broadcasted_iota(jnp.int32, (tk, n_sel_pad), 0) // SEL_BLOCK)
    qpos = qi * tq + lax.broadcasted_iota(jnp.int32, (tk, tq), 1)
    krow = lax.broadcasted_iota(jnp.int32, (tk, tq), 0)

    t_keys = sk_ref.shape[0]

    def load_start(j):
        return pl.multiple_of(jnp.minimum(j * tk, t_keys - tk), tk)

    def tile_scores(j, s_buf, bias_buf):
        expand = jnp.where(rel == j * blocks_per_tile, 1.0, 0.0).astype(BF16)
        picked = jnp.dot(expand, selt, preferred_element_type=F32)
        bias_buf[...] = jnp.where(jnp.logical_and(picked > 0.5, j * tk + krow <= qpos), 0.0, NEG_INF)
        k = sk_ref[pl.ds(load_start(j), tk), :]
        s_buf[...] = lax.dot_general(k, qs, (((1,), (1,)), ((), ())), preferred_element_type=F32)

    def tile_update(j, s_buf, bias_buf):
        vt = svt_ref[:, pl.ds(load_start(j), tk)]
        bias = bias_buf[...]
        ps, alphas = [], []
        for r in range(NSA_GROUP):
            cols = slice(r * tq, (r + 1) * tq)
            s = s_buf[:, cols] + bias
            m_old = m_ref[:, cols]
            m_new = jnp.maximum(m_old, jnp.max(s, axis=0, keepdims=True))
            p = jnp.exp2((s - m_new) * EXP2_SCALE)
            alpha = jnp.exp2((m_old - m_new) * EXP2_SCALE)
            l_ref[:, cols] = alpha * l_ref[:, cols] + jnp.sum(p, axis=0, keepdims=True)
            m_ref[:, cols] = m_new
            ps.append(p.astype(BF16))
            alphas.append(alpha)
        pv = jnp.dot(vt, jnp.concatenate(ps, axis=1), preferred_element_type=F32)
        acc_ref[...] = jnp.concatenate(alphas, axis=1) * acc_ref[...] + pv

    def body(i, carry):
        j = 2 * i
        tile_scores(j + 1, sb_ref, bb_ref)
        tile_update(j, sa_ref, ba_ref)
        tile_scores(j + 2, sa_ref, ba_ref)
        tile_update(j + 1, sb_ref, bb_ref)
        return carry

    n_tiles = ((qi + 1) * tq + tk - 1) // tk
    tile_scores(0, sa_ref, ba_ref)
    lax.fori_loop(0, (n_tiles + 1) // 2, body, 0)
    o_sel = (acc_ref[...] / jnp.maximum(l_ref[...], TINY)).T
    o_ref[...] = _combine_branches(gate_ref[...], ocmp_ref[...], o_sel, o_win, tq).astype(o_ref.dtype)


def _nsa_main_prompt(q_rot_bf, win_bf, rows_bf, sv_t, sel_t, o_cmp, gates, *, tq=128, tk=512):
    t = q_rot_bf.shape[0]
    tq = _pick_tile(t, tq)
    tk = _pick_tile(t, tk)
    assert WINDOW % tq == 0 and tk % SEL_BLOCK == 0
    n_wblk = WINDOW // tq + 1
    n_sel_pad = sel_t.shape[1]
    gw = NSA_GROUP * HEAD_DIM

    def win_spec(s, col0):
        return pl.BlockSpec((tq, HEAD_DIM), lambda g, i: (jnp.maximum(i - (n_wblk - 1) + s, 0), col0 + g))

    in_specs = [pl.BlockSpec((tq, gw), lambda g, i: (i, g))]
    in_specs += [win_spec(s, 0) for s in range(n_wblk)]
    in_specs += [win_spec(s, NSA_KV_HEADS) for s in range(n_wblk)]
    in_specs += [
        pl.BlockSpec((t, HEAD_DIM), lambda g, i: (0, 2 * NSA_KV_HEADS + g)),
        pl.BlockSpec((None, HEAD_DIM, t), lambda g, i: (g, 0, 0)),
        pl.BlockSpec((None, n_sel_pad, tq), lambda g, i: (g, 0, i)),
        pl.BlockSpec((tq, gw), lambda g, i: (i, g)),
        pl.BlockSpec((tq, LANES), lambda g, i: (i, g)),
    ]
    return pl.pallas_call(
        functools.partial(_nsa_main_kernel, tq=tq, tk=tk, n_wblk=n_wblk),
        grid=(NSA_KV_HEADS, t // tq),
        in_specs=in_specs,
        out_specs=pl.BlockSpec((tq, gw), lambda g, i: (i, g)),
        out_shape=jax.ShapeDtypeStruct((t, NSA_Q_HEADS * HEAD_DIM), BF16),
        scratch_shapes=[
            pltpu.VMEM((1, NSA_GROUP * tq), F32),
            pltpu.VMEM((1, NSA_GROUP * tq), F32),
            pltpu.VMEM((HEAD_DIM, NSA_GROUP * tq), F32),
            pltpu.VMEM((tk, NSA_GROUP * tq), F32),
            pltpu.VMEM((tk, NSA_GROUP * tq), F32),
            pltpu.VMEM((tk, tq), F32),
            pltpu.VMEM((tk, tq), F32),
        ],
        compiler_params=_cparams(("parallel", "arbitrary")),
        name="nsa_main_prompt",
    )(q_rot_bf, *([win_bf] * (2 * n_wblk)), rows_bf, sv_t, sel_t, o_cmp, gates)


ROWS_PAD = 16


def _sb_sample_kernel(pt_ref, q_ref, new_ref, cache_ref, u_ref, o_ref, buf_ref, sem, acc_ref, carry_ref,
                      *, n_pages, past, n_new):
    bi = pl.program_id(0)
    qpos = past + n_new - 1
    row = lax.broadcasted_iota(jnp.int32, (ROWS_PAD, HEAD_DIM), 0)
    lane = lax.broadcasted_iota(jnp.int32, (ROWS_PAD, PAGE_SIZE), 1)
    prow = lax.broadcasted_iota(jnp.int32, (PAGE_SIZE, HEAD_DIM), 0)
    q16 = jnp.concatenate([q_ref[...], jnp.zeros((ROWS_PAD - SB_HEADS, HEAD_DIM), F32)], axis=0)
    q_rows = [jnp.where(row == h, q16, 0.0).astype(BF16) for h in range(SB_HEADS)]
    acc_ref[...] = jnp.zeros_like(acc_ref)
    carry_ref[...] = jnp.zeros_like(carry_ref)
    u = u_ref[...]

    def page_copy(j, slot):
        return pltpu.make_async_copy(cache_ref.at[pt_ref[bi, n_pages - 1 - j]], buf_ref.at[slot], sem.at[slot])

    def process(get_k, get_v, kpos0, n_valid):
        z = None
        for h in range(SB_HEADS):
            zh = lax.dot_general(q_rows[h], get_k(h).astype(BF16), (((1,), (1,)), ((), ())),
                                 preferred_element_type=F32)
            z = zh if z is None else z + zh
        mask = jnp.logical_and(kpos0 + lane < qpos, lane < n_valid)
        a, carry = _sb_tile(z * SCALE, u, carry_ref[...], mask)
        for h in range(SB_HEADS):
            acc_ref[h] += jnp.dot(a, get_v(h).astype(BF16), preferred_element_type=F32)
        carry_ref[...] = carry
        return (jnp.max(carry[:SB_HEADS]) < SB_DEAD_LOG).astype(jnp.int32)

    def new_rows(r):
        return jnp.where(prow < n_new, jnp.broadcast_to(new_ref[r:r + 1, :], (PAGE_SIZE, HEAD_DIM)), 0.0)

    page_copy(0, 0).start()
    done0 = process(new_rows, lambda h: new_rows(SB_HEADS + h), past, n_new)

    def cond(state):
        j, done = state
        return jnp.logical_and(j < n_pages, done == 0)

    def body(state):
        j, _ = state
        slot = j % 2
        page_copy(j, slot).wait()

        @pl.when(j + 1 < n_pages)
        def _():
            page_copy(j + 1, 1 - slot).start()

        done = process(lambda h: buf_ref[slot, pl.ds(h, PAGE_SIZE, stride=SB_CACHE_SLOTS), :],
                       lambda h: buf_ref[slot, pl.ds(SB_HEADS + h, PAGE_SIZE, stride=SB_CACHE_SLOTS), :],
                       (n_pages - 1 - j) * PAGE_SIZE, PAGE_SIZE)
        return j + 1, done

    j_end, _ = lax.while_loop(cond, body, (jnp.int32(0), done0))

    @pl.when(j_end < n_pages)
    def _():
        page_copy(j_end, j_end % 2).wait()

    row8 = lax.broadcasted_iota(jnp.int32, (SB_HEADS, HEAD_DIM), 0)
    out = jnp.zeros((SB_HEADS, HEAD_DIM), F32)
    for h in range(SB_HEADS):
        out = jnp.where(row8 == h, acc_ref[h][:SB_HEADS], out)
    o_ref[...] = out


def _sb_sample(q, kv_new, cache, page_table):
    b, n_pages = page_table.shape
    page_rows = PAGE_SIZE * SB_CACHE_SLOTS
    grid_spec = pltpu.PrefetchScalarGridSpec(
        num_scalar_prefetch=1,
        grid=(b,),
        in_specs=[
            pl.BlockSpec((None, SB_HEADS, HEAD_DIM), lambda bi, pt: (bi, 0, 0)),
            pl.BlockSpec((None, SB_CACHE_SLOTS, HEAD_DIM), lambda bi, pt: (bi, 0, 0)),
            pl.BlockSpec(memory_space=pl.ANY),
            pl.BlockSpec((PAGE_SIZE, PAGE_SIZE), lambda bi, pt: (0, 0)),
        ],
        out_specs=pl.BlockSpec((None, SB_HEADS, HEAD_DIM), lambda bi, pt: (bi, 0, 0)),
        scratch_shapes=[
            pltpu.VMEM((2, page_rows, HEAD_DIM), F32),
            pltpu.SemaphoreType.DMA((2,)),
            pltpu.VMEM((SB_HEADS, ROWS_PAD, HEAD_DIM), F32),
            pltpu.VMEM((ROWS_PAD, 1), F32),
        ],
    )
    return pl.pallas_call(
        functools.partial(_sb_sample_kernel, n_pages=n_pages, past=n_pages * PAGE_SIZE, n_new=1),
        grid_spec=grid_spec,
        out_shape=jax.ShapeDtypeStruct((b, SB_HEADS, HEAD_DIM), F32),
        compiler_params=_cparams(("arbitrary",)),
        name="sb_sample",
    )(page_table, q, kv_new, cache, _later_matrix(PAGE_SIZE))


def _nsa_main_sample_kernel(*refs, n_top, past, n_new):
    kh = NSA_KV_HEADS
    idx_ref, pt_ref, q_ref = refs[:3]
    blk_refs = refs[3:3 + kh * n_top]
    new_ref, win_ref, wnew_ref, ocmp_ref, gate_ref, o_ref = refs[3 + kh * n_top:]
    bi = pl.program_id(0)
    qpos = past + n_new - 1
    win_buf = win_ref.shape[0] // WIN_CACHE_SLOTS
    n_keys = n_top * SEL_BLOCK
    lane = lax.broadcasted_iota(jnp.int32, (ROWS_PAD, n_keys), 1)
    seg = lane // SEL_BLOCK

    def new_tile(ref, r, rows):
        row = lax.broadcasted_iota(jnp.int32, (rows, HEAD_DIM), 0)
        return jnp.where(row < n_new, jnp.broadcast_to(ref[r:r + 1, :], (rows, HEAD_DIM)), 0.0)

    def attend(q, k, v, mask):
        s = lax.dot_general(q, k.astype(BF16), (((1,), (1,)), ((), ())), preferred_element_type=F32) * SCALE
        return _masked_softmax_pv(s, mask, v.astype(BF16))

    for g in range(kh):
        q = q_ref[g]
        ks, vs = [], []
        base = jnp.zeros((ROWS_PAD, n_keys), jnp.int32)
        found = jnp.zeros((ROWS_PAD, n_keys), jnp.int32)
        for n in range(n_top):
            blk = idx_ref[(bi * kh + g) * n_top + n]
            is_new = blk * SEL_BLOCK >= past
            ref = blk_refs[g * n_top + n]
            kc = ref[pl.ds(2 * kh + g, SEL_BLOCK, stride=NSA_CACHE_SLOTS), :]
            vc = ref[pl.ds(3 * kh + g, SEL_BLOCK, stride=NSA_CACHE_SLOTS), :]
            ks.append(jnp.where(is_new, new_tile(new_ref, 2 * kh + g, SEL_BLOCK), kc))
            vs.append(jnp.where(is_new, new_tile(new_ref, 3 * kh + g, SEL_BLOCK), vc))
            base = jnp.where(seg == n, blk * SEL_BLOCK, base)
            found = jnp.where(seg == n, (blk >= 0).astype(jnp.int32), found)
        kpos = base + lane % SEL_BLOCK
        mask = jnp.logical_and(found > 0, jnp.logical_and(kpos <= qpos, kpos < past + n_new))
        o_sel = attend(q, jnp.concatenate(ks, axis=0), jnp.concatenate(vs, axis=0), mask)

        wk = jnp.concatenate([win_ref[pl.ds(g, win_buf, stride=WIN_CACHE_SLOTS), :],
                              new_tile(wnew_ref, g, LANES)], axis=0)
        wv = jnp.concatenate([win_ref[pl.ds(kh + g, win_buf, stride=WIN_CACHE_SLOTS), :],
                              new_tile(wnew_ref, kh + g, LANES)], axis=0)
        wlane = lax.broadcasted_iota(jnp.int32, (ROWS_PAD, win_buf + LANES), 1)
        wpos = past - win_buf + wlane
        diff = qpos - wpos
        wmask = jnp.logical_and(jnp.logical_and(diff >= 0, diff <= WINDOW),
                                jnp.logical_and(wpos >= 0, wpos < past + n_new))
        o_win = attend(q, wk, wv, wmask)
        gates = gate_ref[g]
        o_ref[g] = gates[:, 0:1] * ocmp_ref[g] + gates[:, 1:2] * o_sel + gates[:, 2:3] * o_win


def _nsa_main_sample(idx, page_table, q_rot, cache, rows_new, win_state, win_new, o_cmp, gates, *, n_top):
    b, n_pages = page_table.shape
    past = n_pages * PAGE_SIZE
    per_page = PAGE_SIZE // SEL_BLOCK
    kh = NSA_KV_HEADS

    def sel_spec(g, n):
        def index(bi, idx_ref, pt):
            blk = jnp.maximum(idx_ref[(bi * kh + g) * n_top + n], 0)
            page = jnp.minimum(blk // per_page, n_pages - 1)
            return pt[bi, page], blk % per_page, 0
        return pl.BlockSpec((None, SEL_BLOCK * NSA_CACHE_SLOTS, HEAD_DIM), index)

    def per_seq(rows):
        return pl.BlockSpec((None, rows, HEAD_DIM), lambda bi, idx_ref, pt: (bi, 0, 0))

    head_spec = pl.BlockSpec((None, kh, ROWS_PAD, HEAD_DIM), lambda bi, idx_ref, pt: (bi, 0, 0, 0))
    grid_spec = pltpu.PrefetchScalarGridSpec(
        num_scalar_prefetch=2,
        grid=(b,),
        in_specs=[head_spec] + [sel_spec(g, n) for g in range(kh) for n in range(n_top)] + [
            per_seq(NSA_CACHE_SLOTS), per_seq(win_state.shape[1]), per_seq(WIN_CACHE_SLOTS), head_spec, head_spec],
        out_specs=head_spec,
    )
    return pl.pallas_call(
        functools.partial(_nsa_main_sample_kernel, n_top=n_top, past=past, n_new=1),
        grid_spec=grid_spec,
        out_shape=jax.ShapeDtypeStruct((b, kh, ROWS_PAD, HEAD_DIM), F32),
        compiler_params=_cparams(("parallel",)),
        name="nsa_main_sample",
    )(idx, page_table, q_rot, *([cache] * (kh * n_top)), rows_new, win_state, win_new, o_cmp, gates)


def _merge_kernel(osb_ref, onsa_ref, g0_ref, g1_ref, pa_ref, pb_ref, wo_ref, x_ref, out_ref, acc_ref):
    j = pl.program_id(1)

    @pl.when(j == 0)
    def _():
        acc_ref[...] = jnp.zeros_like(acc_ref)

    a = jnp.dot(osb_ref[...], pa_ref[...], preferred_element_type=F32)
    b = jnp.dot(onsa_ref[...], pb_ref[...], preferred_element_type=F32)
    m = (g0_ref[...] * a + g1_ref[...] * b).astype(BF16)
    acc_ref[...] += jnp.dot(m, wo_ref[...], preferred_element_type=F32)

    @pl.when(j == pl.num_programs(1) - 1)
    def _():
        out_ref[...] = x_ref[...] + acc_ref[...]


def _merge(o_sb, o_nsa, mg, pa, pb, wo, x, *, tm_pref=512, tn=512):
    m, d = x.shape
    ka, kb = o_sb.shape[1], o_nsa.shape[1]
    tm = _pick_tile(m, tm_pref)
    tn = _pick_tile(d, tn)
    nj = d // tn
    return pl.pallas_call(
        _merge_kernel,
        grid=(m // tm, nj),
        in_specs=[
            pl.BlockSpec((tm, ka), lambda i, j: (i, 0)),
            pl.BlockSpec((tm, kb), lambda i, j: (i, 0)),
            pl.BlockSpec((tm, tn), lambda i, j: (i, j)),
            pl.BlockSpec((tm, tn), lambda i, j: (i, nj + j)),
            pl.BlockSpec((ka, tn), lambda i, j: (0, j)),
            pl.BlockSpec((kb, tn), lambda i, j: (0, j)),
            pl.BlockSpec((tn, d), lambda i, j: (j, 0)),
            pl.BlockSpec((tm, d), lambda i, j: (i, 0)),
        ],
        out_specs=pl.BlockSpec((tm, d), lambda i, j: (i, 0)),
        out_shape=jax.ShapeDtypeStruct((m, d), F32),
        scratch_shapes=[pltpu.VMEM((tm, d), F32)],
        compiler_params=_cparams(("parallel", "arbitrary")),
        name="merge",
    )(o_sb, o_nsa, mg, mg, pa, pb, wo, x)


SB_Q_COLS = SB_HEADS * HEAD_DIM
SB_KV_COLS = 2 * SB_HEADS * HEAD_DIM
NSA_Q_COLS = NSA_Q_HEADS * HEAD_DIM
NSA_ROW_COLS = 4 * NSA_KV_HEADS * HEAD_DIM
NSA_WIN_COLS = 2 * NSA_KV_HEADS * HEAD_DIM
NSA_GATE_COLS = 3 * NSA_Q_HEADS


def _prep_layer_weights(layer, g_ffn1, ffn1_w_gu, ffn1_w_down, g_mix, w_in, cmp_pe_k, cmp_w_k, cmp_pe_v, cmp_w_v,
                        p_a, p_b, w_o, g_ffn2, ffn2_w_gu, ffn2_w_down):
    w = {}
    w["g1"], w["g_mix"], w["g2"] = g_ffn1[layer][None], g_mix[layer][None], g_ffn2[layer][None]
    w["ffn1"] = _prep_ffn_weights(ffn1_w_gu[layer], ffn1_w_down[layer])
    w["ffn2"] = _prep_ffn_weights(ffn2_w_gu[layer], ffn2_w_down[layer])
    wi = w_in[layer]
    off = 0
    for name, n in (("sbq", SB_Q_COLS), ("sbkv", SB_KV_COLS), ("nq", NSA_Q_COLS), ("rows", NSA_ROW_COLS),
                    ("win", NSA_WIN_COLS)):
        w[name] = wi[:, off:off + n].astype(BF16)
        off += n
    wg = wi[:, off:off + NSA_GATE_COLS].reshape(-1, 3, NSA_KV_HEADS, NSA_GROUP).transpose(0, 2, 1, 3)
    wg = wg.reshape(-1, NSA_KV_HEADS, 3 * NSA_GROUP)
    wg = jnp.pad(wg, ((0, 0), (0, 0), (0, LANES - 3 * NSA_GROUP)))
    w["gate"] = wg.reshape(-1, NSA_KV_HEADS * LANES).astype(BF16)
    off += NSA_GATE_COLS
    w["mg"] = wi[:, off:].astype(BF16)
    w["pe"] = jnp.stack([cmp_pe_k[layer], cmp_pe_v[layer]])
    w["wc"] = jnp.stack([cmp_w_k[layer], cmp_w_v[layer]]).astype(BF16)
    w["pa"], w["pb"], w["wo"] = p_a[layer].astype(BF16), p_b[layer].astype(BF16), w_o[layer].astype(BF16)
    return w


def _mixer_inputs(h, w, cos2, sin2):
    plain_bf = ("plain", None, BF16)
    wide = 4 * LANES
    (sbq,) = _proj(h, w["sbq"], (plain_bf,), tn=wide)
    kv_f, kv_b = _proj(h, w["sbkv"], (("plain", None, F32), plain_bf), tn=wide)
    nq_raw, nq_rot = _proj(h, w["nq"], (plain_bf, ("rope", None, BF16)), cos2, sin2, tn=wide)
    rows_f, rows_b = _proj(h, w["rows"], (("rope", (2,), F32), ("rope", (2,), BF16)), cos2, sin2)
    win_f, win_b = _proj(h, w["win"], (("rope", (0,), F32), ("rope", (0,), BF16)), cos2, sin2)
    (gates,) = _proj(h, w["gate"], (("sigmoid", None, F32),), tn=LANES)
    (mg,) = _proj(h, w["mg"], (("sigmoid", None, F32),), tn=wide)
    return dict(sbq=sbq, kv_f=kv_f, kv_b=kv_b, nq_raw=nq_raw, nq_rot=nq_rot, rows_f=rows_f, rows_b=rows_b,
                win_f=win_f, win_b=win_b, gates=gates, mg=mg)


def _prompt_layer(x, w, cos2, sin2, g_next, last):
    t = x.shape[0]
    assert t % LANES == 0 and t >= CMP_BLOCK
    x1, h = _ffn(x, w["g1"], *w["ffn1"], w["g_mix"], emit_x=True, norm_dtype=BF16)
    mi = _mixer_inputs(h, w, cos2, sin2)
    o_sb = _sb_prompt(mi["sbq"], mi["kv_b"])
    p0, p1 = _compress_prompt(mi["rows_f"], w["pe"], w["wc"])
    n_cmp = (t - CMP_BLOCK) // CMP_STRIDE + 1
    n_sel = -(-t // SEL_BLOCK)
    tq = _pick_tile(t, 128)
    o_cmp, sel_t, _ = _nsa_cmp(mi["nq_raw"][None], p0, p1, tq=tq, qpos0=0, n_cmp=n_cmp, n_sel=n_sel)
    sv_cols = mi["rows_b"][:, 3 * NSA_KV_HEADS * HEAD_DIM:]
    sv_t = sv_cols.reshape(t, NSA_KV_HEADS, HEAD_DIM).transpose(1, 2, 0)
    o_nsa = _nsa_main_prompt(mi["nq_rot"], mi["win_b"], mi["rows_b"], sv_t, sel_t[0], o_cmp[0], mi["gates"])
    x2 = _merge(o_sb, o_nsa, mi["mg"], w["pa"], w["pb"], w["wo"], x1)
    if last:
        (y,) = _ffn(x2, w["g2"], *w["ffn2"], g_next, emit_x=False, norm_dtype=F32)
        x3 = None
    else:
        x3, y = _ffn(x2, w["g2"], *w["ffn2"], g_next, emit_x=True, norm_dtype=F32)
    return x3, y, mi["kv_f"], mi["rows_f"], mi["win_f"]


def _pad_rows(a, rows):
    return jnp.pad(a, ((0, 0), (0, 0), (0, rows - a.shape[2]), (0, 0)))


def _sample_layer(x, w, cos2, sin2, cache_sb, cache_nsa, win_state, page_table, g_next, last):
    b = x.shape[0]
    n_pages = page_table.shape[1]
    past = n_pages * PAGE_SIZE
    total = past + 1
    kh, grp = NSA_KV_HEADS, NSA_GROUP
    x1, h = _ffn(x, w["g1"], *w["ffn1"], w["g_mix"], emit_x=True, norm_dtype=BF16)
    mi = _mixer_inputs(h, w, cos2, sin2)
    o_sb = _sb_sample(mi["sbq"].astype(F32).reshape(b, SB_HEADS, HEAD_DIM),
                      mi["kv_f"].reshape(b, SB_CACHE_SLOTS, HEAD_DIM), cache_sb, page_table)
    o_sb = o_sb.reshape(b, SB_HEADS * HEAD_DIM)

    p0, p1 = _compress_sample(cache_nsa, page_table, w["pe"], w["wc"])
    n_cmp = (total - CMP_BLOCK) // CMP_STRIDE + 1
    n_sel = -(-total // SEL_BLOCK)
    n_top = min(SEL_TOPN, n_sel)
    q_raw = jnp.pad(mi["nq_raw"][:, None, :], ((0, 0), (0, LANES - 1), (0, 0)))
    o_cmp, _, idx = _nsa_cmp(q_raw, p0, p1, tq=LANES, qpos0=past, n_cmp=n_cmp, n_sel=n_sel)
    idx = idx[:, :, :n_top, 0].reshape(-1)
    q_rot = _pad_rows(mi["nq_rot"].reshape(b, kh, grp, HEAD_DIM), ROWS_PAD)
    o_cmp = _pad_rows(o_cmp[:, 0, :].reshape(b, kh, grp, HEAD_DIM), ROWS_PAD)
    gates = mi["gates"].reshape(b, kh, LANES)[:, :, :3 * grp].reshape(b, kh, 3, grp).transpose(0, 1, 3, 2)
    gates = jnp.pad(gates, ((0, 0), (0, 0), (0, ROWS_PAD - grp), (0, LANES - 3)))
    o_nsa = _nsa_main_sample(idx, page_table, q_rot, cache_nsa, mi["rows_f"].reshape(b, NSA_CACHE_SLOTS, HEAD_DIM),
                             win_state, mi["win_f"].reshape(b, WIN_CACHE_SLOTS, HEAD_DIM), o_cmp, gates, n_top=n_top)
    o_nsa = o_nsa[:, :, :grp, :].reshape(b, NSA_Q_HEADS * HEAD_DIM)

    x2 = _merge(o_sb.astype(BF16), o_nsa.astype(BF16), mi["mg"], w["pa"], w["pb"], w["wo"], x1)
    if last:
        (y,) = _ffn(x2, w["g2"], *w["ffn2"], g_next, emit_x=False, norm_dtype=F32)
        x3 = None
    else:
        x3, y = _ffn(x2, w["g2"], *w["ffn2"], g_next, emit_x=True, norm_dtype=F32)
    return x3, y, mi["kv_f"], mi["rows_f"], mi["win_f"]


def kernel(x_prompt, x_sample, cache_sb_kv, cache_nsa_kv, state_win_kv, page_table, g_ffn1, ffn1_w_gu, ffn1_w_down,
           g_mix, w_in, cmp_pe_k, cmp_w_k, cmp_pe_v, cmp_w_v, p_a, p_b, w_o, g_ffn2, ffn2_w_gu, ffn2_w_down, g_final):
    bp, t, d = x_prompt.shape
    db, ds, _ = x_sample.shape
    depth = g_ffn1.shape[0]
    n_pool = cache_sb_kv.shape[1]
    assert bp == 1 and ds == 1
    past = page_table.shape[1] * PAGE_SIZE
    win_buf = state_win_kv.shape[2]
    cos_p, sin_p = _rope_tables(jnp.arange(t))
    cos_s, sin_s = _rope_tables(jnp.full((db,), past, jnp.int32))
    g_fin = g_final[None]

    xp = x_prompt.reshape(t, d)
    xs = x_sample.reshape(db, d)
    sb_p, nsa_p, win_p, sb_s, nsa_s, win_s = [], [], [], [], [], []
    yp = ys = None
    for layer in range(depth):
        w = _prep_layer_weights(layer, g_ffn1, ffn1_w_gu, ffn1_w_down, g_mix, w_in, cmp_pe_k, cmp_w_k, cmp_pe_v,
                                cmp_w_v, p_a, p_b, w_o, g_ffn2, ffn2_w_gu, ffn2_w_down)
        last = layer == depth - 1
        xp, yp, kv_f, rows_f, win_f = _prompt_layer(xp, w, cos_p, sin_p, g_fin, last)
        sb_p.append(kv_f.reshape(1, t, 2, SB_HEADS, HEAD_DIM))
        nsa_p.append(rows_f.reshape(1, t, 4, NSA_KV_HEADS, HEAD_DIM))
        wp = min(WINDOW, t)
        win_p.append(win_f[t - wp:].reshape(1, wp, 2, NSA_KV_HEADS, HEAD_DIM))

        cache_sb = cache_sb_kv[layer].reshape(n_pool, PAGE_SIZE * SB_CACHE_SLOTS, HEAD_DIM)
        cache_nsa = cache_nsa_kv[layer].reshape(n_pool, PAGE_SIZE * NSA_CACHE_SLOTS, HEAD_DIM)
        win_state = state_win_kv[layer].reshape(db, win_buf * WIN_CACHE_SLOTS, HEAD_DIM)
        xs, ys, kv_f, rows_f, win_f = _sample_layer(xs, w, cos_s, sin_s, cache_sb, cache_nsa, win_state, page_table,
                                                    g_fin, last)
        sb_s.append(kv_f.reshape(db, 1, 2, SB_HEADS, HEAD_DIM))
        nsa_s.append(rows_f.reshape(db, 1, 4, NSA_KV_HEADS, HEAD_DIM))
        win_all = jnp.concatenate([state_win_kv[layer], win_f.reshape(db, 1, 2, NSA_KV_HEADS, HEAD_DIM)], axis=1)
        win_s.append(win_all[:, win_all.shape[1] - win_buf:])
    return (yp.reshape(1, t, d), ys.reshape(db, 1, d), jnp.stack(sb_p), jnp.stack(nsa_p), jnp.stack(win_p),
            jnp.stack(sb_s), jnp.stack(nsa_s), jnp.stack(win_s))
```

```python
import functools

import numpy as np
import jax
import jax.numpy as jnp
from jax import lax
from jax.experimental import pallas as pl
from jax.experimental.pallas import tpu as pltpu

HEAD_DIM = 128
SB_HEADS = 8
NSA_Q_HEADS = 8
NSA_KV_HEADS = 2
NSA_GROUP = NSA_Q_HEADS // NSA_KV_HEADS
CMP_BLOCK = 32
CMP_STRIDE = 16
SEL_BLOCK = 64
SEL_TOPN = 16
WINDOW = 512
PAGE_SIZE = 128
ROPE_THETA = 10000.0
NORM_EPS = 1e-6
HALF_STEP = 0.5
SCALE = HEAD_DIM ** -0.5
NEG_INF = -1e30
FORCED_SCORE = 1e4
TINY = 1e-30

SB_DEAD_LOG = -104.0

EXP2_SCALE = SCALE * float(np.log2(np.e))
MAX_FLOOR = 0.1 * NEG_INF

LANES = 128
SB_CACHE_SLOTS = 2 * SB_HEADS
NSA_CACHE_SLOTS = 4 * NSA_KV_HEADS
WIN_CACHE_SLOTS = 2 * NSA_KV_HEADS
VMEM_LIMIT = 56 * 1024 * 1024

F32 = jnp.float32
BF16 = jnp.bfloat16


def _cparams(sem):
    return pltpu.CompilerParams(dimension_semantics=sem, vmem_limit_bytes=VMEM_LIMIT)


def _round_up(x, m):
    return -(-x // m) * m


def _pick_tile(n, pref):
    t = min(pref, n)
    while n % t:
        t //= 2
    return t


def _rms(x, g):
    return x * lax.rsqrt(jnp.mean(x * x, axis=-1, keepdims=True) + NORM_EPS) * g


def _sigmoid(x):
    return 1.0 / (1.0 + jnp.exp(-x))


def _ffn_kernel(x_ref, g_ref, wg_ref, wu_ref, wd_ref, g2_ref, *refs, emit_x):
    if emit_x:
        out_ref, n_ref, h_scr, acc_scr = refs
    else:
        n_ref, h_scr, acc_scr = refs
    j = pl.program_id(1)

    @pl.when(j == 0)
    def _():
        h_scr[...] = _rms(x_ref[...], g_ref[...]).astype(BF16)
        acc_scr[...] = jnp.zeros_like(acc_scr)

    h = h_scr[...]
    gate = jnp.dot(h, wg_ref[...], preferred_element_type=F32)
    up = jnp.dot(h, wu_ref[...], preferred_element_type=F32)
    act = (gate * _sigmoid(gate) * up).astype(BF16)
    acc_scr[...] += jnp.dot(act, wd_ref[...], preferred_element_type=F32)

    @pl.when(j == pl.num_programs(1) - 1)
    def _():
        y = x_ref[...] + HALF_STEP * acc_scr[...]
        if emit_x:
            out_ref[...] = y
        n_ref[...] = _rms(y, g2_ref[...]).astype(n_ref.dtype)


def _ffn(x, g, wg, wu, wd, g2, *, emit_x, norm_dtype, tm_pref=512, tf=512):
    m, d = x.shape
    fpad = wg.shape[1]
    tm = _pick_tile(m, tm_pref)
    grid = (m // tm, fpad // tf)
    out_shape = []
    out_specs = []
    if emit_x:
        out_shape.append(jax.ShapeDtypeStruct((m, d), F32))
        out_specs.append(pl.BlockSpec((tm, d), lambda i, j: (i, 0)))
    out_shape.append(jax.ShapeDtypeStruct((m, d), norm_dtype))
    out_specs.append(pl.BlockSpec((tm, d), lambda i, j: (i, 0)))
    return pl.pallas_call(
        functools.partial(_ffn_kernel, emit_x=emit_x),
        grid=grid,
        in_specs=[
            pl.BlockSpec((tm, d), lambda i, j: (i, 0)),
            pl.BlockSpec((1, d), lambda i, j: (0, 0)),
            pl.BlockSpec((d, tf), lambda i, j: (0, j)),
            pl.BlockSpec((d, tf), lambda i, j: (0, j)),
            pl.BlockSpec((tf, d), lambda i, j: (j, 0)),
            pl.BlockSpec((1, d), lambda i, j: (0, 0)),
        ],
        out_specs=out_specs,
        out_shape=out_shape,
        scratch_shapes=[pltpu.VMEM((tm, d), BF16), pltpu.VMEM((tm, d), F32)],
        compiler_params=_cparams(("parallel", "arbitrary")),
        name="ffn",
    )(x, g, wg, wu, wd, g2)


def _prep_ffn_weights(w_gu, w_down, tf=512):
    d, two_f = w_gu.shape
    f = two_f // 2
    fpad = _round_up(f, tf)
    wg = jnp.pad(w_gu[:, :f].astype(BF16), ((0, 0), (0, fpad - f)))
    wu = jnp.pad(w_gu[:, f:].astype(BF16), ((0, 0), (0, fpad - f)))
    wd = jnp.pad(w_down.astype(BF16), ((0, fpad - f), (0, 0)))
    return wg, wu, wd


def _rope_tile(x, cos2, sin2):
    return x * cos2 + pltpu.roll(x, HEAD_DIM // 2, axis=1) * sin2


def _proj_kernel(a_ref, w_ref, *refs, outs, use_rope):
    if use_rope:
        cos_ref, sin_ref = refs[:2]
        refs = refs[2:]
    j = pl.program_id(1)
    acc = jnp.dot(a_ref[...], w_ref[...], preferred_element_type=F32)
    tn = acc.shape[1]
    for (mode, rope_tiles, _), o_ref in zip(outs, refs):
        if mode == "sigmoid":
            o_ref[...] = _sigmoid(acc).astype(o_ref.dtype)
        elif mode == "rope":
            cos2 = cos_ref[...]
            sin2 = sin_ref[...]
            rot = jnp.concatenate(
                [_rope_tile(acc[:, c * LANES:(c + 1) * LANES], cos2, sin2) for c in range(tn // LANES)], axis=1)
            if rope_tiles is None:
                o_ref[...] = rot.astype(o_ref.dtype)
            else:
                is_rope = functools.reduce(jnp.logical_or, [j == t for t in rope_tiles])
                o_ref[...] = jnp.where(is_rope, rot, acc).astype(o_ref.dtype)
        else:
            o_ref[...] = acc.astype(o_ref.dtype)


def _proj(a, w, outs, cos2=None, sin2=None, *, tm_pref=1024, tn=256):
    m, k = a.shape
    n = w.shape[1]
    tm = _pick_tile(m, tm_pref)
    tn = min(tn, n)
    use_rope = any(mode == "rope" for mode, _, _ in outs)
    in_specs = [pl.BlockSpec((tm, k), lambda i, j: (i, 0)), pl.BlockSpec((k, tn), lambda i, j: (0, j))]
    args = [a, w]
    if use_rope:
        in_specs += [pl.BlockSpec((tm, LANES), lambda i, j: (i, 0))] * 2
        args += [cos2, sin2]
    res = pl.pallas_call(
        functools.partial(_proj_kernel, outs=outs, use_rope=use_rope),
        grid=(m // tm, n // tn),
        in_specs=in_specs,
        out_specs=[pl.BlockSpec((tm, tn), lambda i, j: (i, j)) for _ in outs],
        out_shape=[jax.ShapeDtypeStruct((m, n), dt) for _, _, dt in outs],
        compiler_params=_cparams(("parallel", "arbitrary")),
        name="proj",
    )(*args)
    return res


def _rope_tables(pos):
    half = HEAD_DIM // 2
    inv_freq = ROPE_THETA ** (-2.0 * jnp.arange(half, dtype=F32) / HEAD_DIM)
    ang = pos.astype(F32)[:, None] * inv_freq[None, :]
    cos, sin = jnp.cos(ang), jnp.sin(ang)
    return jnp.concatenate([cos, cos], axis=1), jnp.concatenate([-sin, sin], axis=1)


def _later_matrix(tk):
    j = np.arange(tk)[:, None]
    s = np.arange(tk)[None, :]
    return jnp.asarray((j > s).astype(np.float32), dtype=BF16)


def _sb_tile(z, u, carry, mask):
    soft = jnp.log(1.0 + jnp.exp(-jnp.abs(z)))
    log_beta = jnp.minimum(z, 0.0) - soft
    log_keep = jnp.where(mask, jnp.minimum(-z, 0.0) - soft, 0.0)
    hi = log_keep.astype(BF16)
    lo = (log_keep - hi.astype(F32)).astype(BF16)
    later = (jnp.dot(hi, u, preferred_element_type=F32) + jnp.dot(lo, u, preferred_element_type=F32)) + carry
    a = jnp.where(mask, jnp.exp(log_beta + later), 0.0)
    return a.astype(BF16), carry + jnp.sum(log_keep, axis=1, keepdims=True)


def _sb_prompt_kernel(q_ref, k_ref, v_ref, u_ref, o_ref, acc_ref, carry_ref, *, tq, tk):
    qi = pl.program_id(1)
    acc_ref[...] = jnp.zeros_like(acc_ref)
    carry_ref[...] = jnp.zeros_like(carry_ref)
    q = q_ref[...]
    u = u_ref[...]
    qpos = qi * tq + lax.broadcasted_iota(jnp.int32, (tq, tk), 0)
    lane = lax.broadcasted_iota(jnp.int32, (tq, tk), 1)

    end = (qi + 1) * tq

    def cond(state):
        n, done = state
        return jnp.logical_and(end - n * tk > 0, done == 0)

    def body(state):
        n, _ = state
        hi = end - n * tk
        start = pl.multiple_of(jnp.maximum(hi - tk, 0), tq)
        k = k_ref[pl.ds(start, tk), :]
        v = v_ref[pl.ds(start, tk), :]
        z = lax.dot_general(q, k, (((1,), (1,)), ((), ())), preferred_element_type=F32) * SCALE
        kpos = start + lane
        mask = jnp.logical_and(kpos < qpos, kpos < hi)
        a, carry = _sb_tile(z, u, carry_ref[...], mask)
        acc_ref[...] += jnp.dot(a, v, preferred_element_type=F32)
        carry_ref[...] = carry
        done = (jnp.max(carry) < SB_DEAD_LOG).astype(jnp.int32)
        return n + 1, done

    lax.while_loop(cond, body, (jnp.int32(0), jnp.int32(0)))
    o_ref[...] = acc_ref[...].astype(o_ref.dtype)


def _sb_prompt(q_bf, kv_bf, *, tq=256, tk=512):
    t = q_bf.shape[0]
    tq = _pick_tile(t, tq)
    tk = _pick_tile(t, tk)
    assert tk % tq == 0
    return pl.pallas_call(
        functools.partial(_sb_prompt_kernel, tq=tq, tk=tk),
        grid=(SB_HEADS, t // tq),
        in_specs=[
            pl.BlockSpec((tq, HEAD_DIM), lambda h, i: (i, h)),
            pl.BlockSpec((t, HEAD_DIM), lambda h, i: (0, h)),
            pl.BlockSpec((t, HEAD_DIM), lambda h, i: (0, SB_HEADS + h)),
            pl.BlockSpec((tk, tk), lambda h, i: (0, 0)),
        ],
        out_specs=pl.BlockSpec((tq, HEAD_DIM), lambda h, i: (i, h)),
        out_shape=jax.ShapeDtypeStruct((t, SB_HEADS * HEAD_DIM), BF16),
        scratch_shapes=[pltpu.VMEM((tq, HEAD_DIM), F32), pltpu.VMEM((tq, 1), F32)],
        compiler_params=_cparams(("parallel", "arbitrary")),
        name="sb_prompt",
    )(q_bf, kv_bf, kv_bf, _later_matrix(tk))


def _compress_kernel(*refs, n_in, rows, prefetch):
    if prefetch:
        refs = refs[1:]
    in_refs = refs[:n_in]
    pe_ref, w_ref, p0_ref, p1_ref = refs[n_in:n_in + 4]
    col_refs = refs[n_in + 4:]
    nch = rows // CMP_STRIDE
    for sg in range(2 * NSA_KV_HEADS):
        slot = sg // NSA_KV_HEADS
        col_ref = col_refs[sg]
        for i, r in enumerate(in_refs):
            if prefetch:
                col_ref[i * rows:(i + 1) * rows, :] = r[pl.ds(sg, rows, stride=NSA_CACHE_SLOTS), :]
            else:
                col_ref[i * rows:(i + 1) * rows, :] = r[:, sg * LANES:(sg + 1) * LANES]
        acc0 = jnp.zeros((n_in * nch, HEAD_DIM), F32)
        acc1 = jnp.zeros((n_in * nch, HEAD_DIM), F32)
        for j in range(CMP_STRIDE):
            x = col_ref[pl.ds(j, n_in * nch, stride=CMP_STRIDE), :]
            x0 = (x + pe_ref[slot, pl.ds(j, 1), :]).astype(BF16)
            x1 = (x + pe_ref[slot, pl.ds(CMP_STRIDE + j, 1), :]).astype(BF16)
            acc0 += jnp.dot(x0, w_ref[slot, j], preferred_element_type=F32)
            acc1 += jnp.dot(x1, w_ref[slot, CMP_STRIDE + j], preferred_element_type=F32)
        p0_ref[sg] = acc0
        p1_ref[sg] = acc1


def _compress_prompt(rows_f32, pe, w, *, rows_pref=2048):
    t = rows_f32.shape[0]
    rows = _pick_tile(t, rows_pref)
    nch = rows // CMP_STRIDE
    n_sg = 2 * NSA_KV_HEADS
    out_sds = jax.ShapeDtypeStruct((1, n_sg, t // CMP_STRIDE, HEAD_DIM), F32)
    out_spec = pl.BlockSpec((None, n_sg, nch, HEAD_DIM), lambda i: (0, 0, i, 0))
    return pl.pallas_call(
        functools.partial(_compress_kernel, n_in=1, rows=rows, prefetch=False),
        grid=(t // rows,),
        in_specs=[
            pl.BlockSpec((rows, n_sg * LANES), lambda i: (i, 0)),
            pl.BlockSpec(pe.shape, lambda i: (0, 0, 0)),
            pl.BlockSpec(w.shape, lambda i: (0, 0, 0, 0)),
        ],
        out_specs=[out_spec, out_spec],
        out_shape=[out_sds, out_sds],
        scratch_shapes=[pltpu.VMEM((rows, HEAD_DIM), F32)] * n_sg,
        compiler_params=_cparams(("parallel",)),
        name="compress_prompt",
    )(rows_f32, pe, w)


def _compress_sample(cache, page_table, pe, w, *, pages_per_step=32):
    b, n_pages = page_table.shape
    npp = _pick_tile(n_pages, pages_per_step)
    nch = PAGE_SIZE // CMP_STRIDE
    n_sg = 2 * NSA_KV_HEADS
    out_sds = jax.ShapeDtypeStruct((b, n_sg, n_pages * nch, HEAD_DIM), F32)
    out_spec = pl.BlockSpec((None, n_sg, npp * nch, HEAD_DIM), lambda bi, i, pt: (bi, 0, i, 0))

    def page_spec(k):
        return pl.BlockSpec((None, PAGE_SIZE * NSA_CACHE_SLOTS, HEAD_DIM),
                            lambda bi, i, pt: (pt[bi, i * npp + k], 0, 0))

    grid_spec = pltpu.PrefetchScalarGridSpec(
        num_scalar_prefetch=1,
        grid=(b, n_pages // npp),
        in_specs=[page_spec(k) for k in range(npp)] + [
            pl.BlockSpec(pe.shape, lambda bi, i, pt: (0, 0, 0)),
            pl.BlockSpec(w.shape, lambda bi, i, pt: (0, 0, 0, 0)),
        ],
        out_specs=[out_spec, out_spec],
        scratch_shapes=[pltpu.VMEM((npp * PAGE_SIZE, HEAD_DIM), F32)] * n_sg,
    )
    return pl.pallas_call(
        functools.partial(_compress_kernel, n_in=npp, rows=PAGE_SIZE, prefetch=True),
        grid_spec=grid_spec,
        out_shape=[out_sds, out_sds],
        compiler_params=_cparams(("parallel", "arbitrary")),
        name="compress_sample",
    )(page_table, *([cache] * npp), pe, w)


def _selection_matrix(n_c, n_cmp, n_sel, n_sel_pad):
    ratio = SEL_BLOCK // CMP_STRIDE
    lo = -((CMP_BLOCK - 1) // CMP_STRIDE)
    hi = (SEL_BLOCK - 1) // CMP_STRIDE
    c = np.arange(n_c)[None, :]
    b = np.arange(n_sel_pad)[:, None]
    hit = (c >= ratio * b + lo) & (c <= ratio * b + hi) & (c < n_cmp) & (b < n_sel)
    return jnp.asarray(hit.astype(np.float32), dtype=BF16)


def _split_dot(m01, x):
    hi = x.astype(BF16)
    lo = (x - hi.astype(F32)).astype(BF16)
    return jnp.dot(m01, hi, preferred_element_type=F32) + jnp.dot(m01, lo, preferred_element_type=F32)


def _nsa_cmp_kernel(q_ref, p0k_ref, p1k_ref, p0v_ref, p1v_ref, smap_ref, ocmp_ref, sel_ref, idx_ref, ck_ref, cvt_ref,
                    *, tq, qpos0, n_sel, n_top):
    qi = pl.program_id(2)
    n_c = p0k_ref.shape[0]
    n_sel_pad = smap_ref.shape[0]

    @pl.when(qi == 0)
    def _():
        ck_ref[...] = (p0k_ref[...] + pltpu.roll(p1k_ref[...], n_c - 1, axis=0)).astype(BF16)
        cvt_ref[...] = (p0v_ref[...] + pltpu.roll(p1v_ref[...], n_c - 1, axis=0)).T.astype(BF16)

    q = q_ref[...]
    ck = ck_ref[...]
    cvt = cvt_ref[...]
    qpos_c = qpos0 + qi * tq + lax.broadcasted_iota(jnp.int32, (n_c, tq), 1)
    cend = lax.broadcasted_iota(jnp.int32, (n_c, tq), 0) * CMP_STRIDE + (CMP_BLOCK - 1)
    bias = jnp.where(cend <= qpos_c, 0.0, NEG_INF)
    imp = None
    outs = []
    for r in range(NSA_GROUP):
        s = lax.dot_general(ck, q[:, r * LANES:(r + 1) * LANES], (((1,), (1,)), ((), ())),
                            preferred_element_type=F32) + bias
        m = jnp.maximum(jnp.max(s, axis=0, keepdims=True), MAX_FLOOR)
        p = jnp.exp2((s - m) * EXP2_SCALE)
        p = p * (1.0 / jnp.maximum(jnp.sum(p, axis=0, keepdims=True), TINY))
        outs.append(jnp.dot(cvt, p.astype(BF16), preferred_element_type=F32))
        imp = p if imp is None else imp + p
    o = jnp.concatenate(outs, axis=1).T
    ocmp_ref[...] = jnp.concatenate([o[r * tq:(r + 1) * tq] for r in range(NSA_GROUP)], axis=1)

    imp_sel = _split_dot(smap_ref[...], imp)
    qpos = qpos0 + qi * tq + lax.broadcasted_iota(jnp.int32, (n_sel_pad, tq), 1)
    blk = lax.broadcasted_iota(jnp.int32, (n_sel_pad, tq), 0)
    cur = qpos // SEL_BLOCK
    eligible = jnp.logical_and(blk * SEL_BLOCK <= qpos, blk < n_sel)
    forced = jnp.logical_or(blk == 0, jnp.logical_or(blk == cur, blk == cur - 1))
    score = jnp.where(eligible, jnp.where(forced, FORCED_SCORE, imp_sel), NEG_INF)
    blkf = blk.astype(F32)
    pick = lax.broadcasted_iota(jnp.int32, (idx_ref.shape[0], tq), 0)
    selected = jnp.zeros((n_sel_pad, tq), F32)
    idx_acc = jnp.full(pick.shape, -1.0, F32)
    for i in range(n_top):
        mx = jnp.max(score, axis=0, keepdims=True)
        first = jnp.min(jnp.where(score == mx, blkf, 1e9), axis=0, keepdims=True)
        valid = mx > 0.5 * NEG_INF
        hit = blkf == first
        selected = jnp.where(jnp.logical_and(hit, valid), 1.0, selected)
        idx_acc = jnp.where(pick == i, jnp.where(valid, first, -1.0), idx_acc)
        score = jnp.where(hit, -3e38, score)
    sel_ref[...] = selected.astype(sel_ref.dtype)
    idx_ref[...] = idx_acc.astype(jnp.int32)


def _nsa_cmp(q_bf, p0, p1, *, tq, qpos0, n_cmp, n_sel):
    b, t, _ = q_bf.shape
    n_c = p0.shape[2]
    n_sel_pad = _round_up(n_sel, LANES)
    n_top = min(SEL_TOPN, n_sel)
    pick_rows = _round_up(n_top, 8)
    smap = _selection_matrix(n_c, n_cmp, n_sel, n_sel_pad)
    gw = NSA_GROUP * HEAD_DIM

    def part_spec(slot):
        return pl.BlockSpec((None, None, n_c, HEAD_DIM), lambda bi, g, i: (bi, slot * NSA_KV_HEADS + g, 0, 0))

    return pl.pallas_call(
        functools.partial(_nsa_cmp_kernel, tq=tq, qpos0=qpos0, n_sel=n_sel, n_top=n_top),
        grid=(b, NSA_KV_HEADS, t // tq),
        in_specs=[
            pl.BlockSpec((None, tq, gw), lambda bi, g, i: (bi, i, g)),
            part_spec(0), part_spec(0), part_spec(1), part_spec(1),
            pl.BlockSpec((n_sel_pad, n_c), lambda bi, g, i: (0, 0)),
        ],
        out_specs=[
            pl.BlockSpec((None, tq, gw), lambda bi, g, i: (bi, i, g)),
            pl.BlockSpec((None, None, n_sel_pad, tq), lambda bi, g, i: (bi, g, 0, i)),
            pl.BlockSpec((None, None, pick_rows, tq), lambda bi, g, i: (bi, g, 0, i)),
        ],
        out_shape=[
            jax.ShapeDtypeStruct((b, t, NSA_Q_HEADS * HEAD_DIM), F32),
            jax.ShapeDtypeStruct((b, NSA_KV_HEADS, n_sel_pad, t), F32),
            jax.ShapeDtypeStruct((b, NSA_KV_HEADS, pick_rows, t), jnp.int32),
        ],
        scratch_shapes=[pltpu.VMEM((n_c, HEAD_DIM), BF16), pltpu.VMEM((HEAD_DIM, n_c), BF16)],
        compiler_params=_cparams(("arbitrary", "arbitrary", "arbitrary")),
        name="nsa_cmp",
    )(q_bf, p0, p1, p0, p1, smap)


def _stack_heads(q):
    return jnp.concatenate([q[:, r * LANES:(r + 1) * LANES] for r in range(NSA_GROUP)], axis=0)


def _masked_softmax_pv(s, mask, v):
    s = jnp.where(mask, s, NEG_INF)
    p = jnp.where(mask, jnp.exp(s - jnp.max(s, axis=1, keepdims=True)), 0.0)
    p = p / jnp.maximum(jnp.sum(p, axis=1, keepdims=True), TINY)
    return jnp.dot(p.astype(BF16), v, preferred_element_type=F32)


def _combine_branches(gates, o_cmp, o_sel, o_win, tq):
    outs = []
    for r in range(NSA_GROUP):
        rows = slice(r * tq, (r + 1) * tq)
        outs.append(gates[:, r:r + 1] * o_cmp[:, r * LANES:(r + 1) * LANES]
                    + gates[:, NSA_GROUP + r:NSA_GROUP + r + 1] * o_sel[rows]
                    + gates[:, 2 * NSA_GROUP + r:2 * NSA_GROUP + r + 1] * o_win[rows])
    return jnp.concatenate(outs, axis=1)


def _nsa_main_kernel(*refs, tq, tk, n_wblk):
    q_ref = refs[0]
    wk_refs = refs[1:1 + n_wblk]
    wv_refs = refs[1 + n_wblk:1 + 2 * n_wblk]
    (sk_ref, svt_ref, sel_ref, ocmp_ref, gate_ref, o_ref, m_ref, l_ref, acc_ref,
     sa_ref, sb_ref, ba_ref, bb_ref) = refs[1 + 2 * n_wblk:]
    qi = pl.program_id(1)
    qs = _stack_heads(q_ref[...])
    n_sel_pad = sel_ref.shape[0]

    wk = jnp.concatenate([r[...] for r in wk_refs], axis=0)
    wv = jnp.concatenate([r[...] for r in wv_refs], axis=0)
    wlen = n_wblk * tq
    qpos_w = qi * tq + lax.broadcasted_iota(jnp.int32, (tq, wlen), 0)
    wpos = (qi - (n_wblk - 1)) * tq + lax.broadcasted_iota(jnp.int32, (tq, wlen), 1)
    diff = qpos_w - wpos
    wm1 = jnp.logical_and(jnp.logical_and(diff >= 0, diff <= WINDOW), wpos >= 0)
    s_w = lax.dot_general(qs, wk, (((1,), (1,)), ((), ())), preferred_element_type=F32) * SCALE
    o_win = _masked_softmax_pv(s_w, jnp.concatenate([wm1] * NSA_GROUP, axis=0), wv)

    m_ref[...] = jnp.full_like(m_ref, MAX_FLOOR)
    l_ref[...] = jnp.zeros_like(l_ref)
    acc_ref[...] = jnp.zeros_like(acc_ref)
    blocks_per_tile = tk // SEL_BLOCK
    qpos = qi * tq + lax.broadcasted_iota(jnp.int32, (tk, tq), 1)
    krow = lax.broadcasted_iota(jnp.int32, (tk, tq), 0)

    t_keys = sk_ref.shape[0]

    def load_start(j):
        return pl.multiple_of(jnp.minimum(j * tk, t_keys - tk), tk)

    def tile_scores(j, s_buf, bias_buf):
        blk0 = pl.multiple_of(jnp.minimum(j * blocks_per_tile, n_sel_pad - blocks_per_tile), blocks_per_tile)
        sel_rows = sel_ref[pl.ds(blk0, blocks_per_tile), :]
        picked = jnp.concatenate([jnp.broadcast_to(sel_rows[b:b + 1, :], (SEL_BLOCK, tq))
                                  for b in range(blocks_per_tile)], axis=0)
        bias_buf[...] = jnp.where(jnp.logical_and(picked > 0.5, j * tk + krow <= qpos), 0.0, NEG_INF)
        k = sk_ref[pl.ds(load_start(j), tk), :]
        s_buf[...] = lax.dot_general(k, qs, (((1,), (1,)), ((), ())), preferred_element_type=F32)

    def tile_update(j, s_buf, bias_buf):
        vt = svt_ref[:, pl.ds(load_start(j), tk)]
        bias = bias_buf[...]
        ps, alphas = [], []
        for r in range(NSA_GROUP):
            cols = slice(r * tq, (r + 1) * tq)
            s = s_buf[:, cols] + bias
            m_old = m_ref[:, cols]
            m_new = jnp.maximum(m_old, jnp.max(s, axis=0, keepdims=True))
            p = jnp.exp2((s - m_new) * EXP2_SCALE)
            alpha = jnp.exp2((m_old - m_new) * EXP2_SCALE)
            l_ref[:, cols] = alpha * l_ref[:, cols] + jnp.sum(p, axis=0, keepdims=True)
            m_ref[:, cols] = m_new
            ps.append(p.astype(BF16))
            alphas.append(alpha)
        pv = jnp.dot(vt, jnp.concatenate(ps, axis=1), preferred_element_type=F32)
        acc_ref[...] = jnp.concatenate(alphas, axis=1) * acc_ref[...] + pv

    def body(i, carry):
        j = 2 * i
        tile_scores(j + 1, sb_ref, bb_ref)
        tile_update(j, sa_ref, ba_ref)
        tile_scores(j + 2, sa_ref, ba_ref)
        tile_update(j + 1, sb_ref, bb_ref)
        return carry

    n_tiles = ((qi + 1) * tq + tk - 1) // tk
    tile_scores(0, sa_ref, ba_ref)
    lax.fori_loop(0, (n_tiles + 1) // 2, body, 0)
    o_sel = (acc_ref[...] / jnp.maximum(l_ref[...], TINY)).T
    o_ref[...] = _combine_branches(gate_ref[...], ocmp_ref[...], o_sel, o_win, tq).astype(o_ref.dtype)


def _nsa_main_prompt(q_rot_bf, win_bf, rows_bf, sv_t, sel_t, o_cmp, gates, *, tq=128, tk=512):
    t = q_rot_bf.shape[0]
    tq = _pick_tile(t, tq)
    tk = _pick_tile(t, tk)
    assert WINDOW % tq == 0 and tk % (8 * SEL_BLOCK) == 0
    n_wblk = WINDOW // tq + 1
    n_sel_pad = sel_t.shape[1]
    gw = NSA_GROUP * HEAD_DIM

    def win_spec(s, col0):
        return pl.BlockSpec((tq, HEAD_DIM), lambda g, i: (jnp.maximum(i - (n_wblk - 1) + s, 0), col0 + g))

    in_specs = [pl.BlockSpec((tq, gw), lambda g, i: (i, g))]
    in_specs += [win_spec(s, 0) for s in range(n_wblk)]
    in_specs += [win_spec(s, NSA_KV_HEADS) for s in range(n_wblk)]
    in_specs += [
        pl.BlockSpec((t, HEAD_DIM), lambda g, i: (0, 2 * NSA_KV_HEADS + g)),
        pl.BlockSpec((None, HEAD_DIM, t), lambda g, i: (g, 0, 0)),
        pl.BlockSpec((None, n_sel_pad, tq), lambda g, i: (g, 0, i)),
        pl.BlockSpec((tq, gw), lambda g, i: (i, g)),
        pl.BlockSpec((tq, LANES), lambda g, i: (i, g)),
    ]
    return pl.pallas_call(
        functools.partial(_nsa_main_kernel, tq=tq, tk=tk, n_wblk=n_wblk),
        grid=(NSA_KV_HEADS, t // tq),
        in_specs=in_specs,
        out_specs=pl.BlockSpec((tq, gw), lambda g, i: (i, g)),
        out_shape=jax.ShapeDtypeStruct((t, NSA_Q_HEADS * HEAD_DIM), BF16),
        scratch_shapes=[
            pltpu.VMEM((1, NSA_GROUP * tq), F32),
            pltpu.VMEM((1, NSA_GROUP * tq), F32),
            pltpu.VMEM((HEAD_DIM, NSA_GROUP * tq), F32),
            pltpu.VMEM((tk, NSA_GROUP * tq), F32),
            pltpu.VMEM((tk, NSA_GROUP * tq), F32),
            pltpu.VMEM((tk, tq), F32),
            pltpu.VMEM((tk, tq), F32),
        ],
        compiler_params=_cparams(("parallel", "arbitrary")),
        name="nsa_main_prompt",
    )(q_rot_bf, *([win_bf] * (2 * n_wblk)), rows_bf, sv_t, sel_t, o_cmp, gates)


ROWS_PAD = 16


def _sb_sample_kernel(pt_ref, q_ref, new_ref, cache_ref, u_ref, o_ref, buf_ref, sem, acc_ref, carry_ref,
                      *, n_pages, past, n_new):
    bi = pl.program_id(0)
    qpos = past + n_new - 1
    row = lax.broadcasted_iota(jnp.int32, (ROWS_PAD, HEAD_DIM), 0)
    lane = lax.broadcasted_iota(jnp.int32, (ROWS_PAD, PAGE_SIZE), 1)
    prow = lax.broadcasted_iota(jnp.int32, (PAGE_SIZE, HEAD_DIM), 0)
    q16 = jnp.concatenate([q_ref[...], jnp.zeros((ROWS_PAD - SB_HEADS, HEAD_DIM), F32)], axis=0)
    q_rows = [jnp.where(row == h, q16, 0.0).astype(BF16) for h in range(SB_HEADS)]
    acc_ref[...] = jnp.zeros_like(acc_ref)
    carry_ref[...] = jnp.zeros_like(carry_ref)
    u = u_ref[...]

    def page_copy(j, slot):
        return pltpu.make_async_copy(cache_ref.at[pt_ref[bi, n_pages - 1 - j]], buf_ref.at[slot], sem.at[slot])

    def process(get_k, get_v, kpos0, n_valid):
        z = None
        for h in range(SB_HEADS):
            zh = lax.dot_general(q_rows[h], get_k(h).astype(BF16), (((1,), (1,)), ((), ())),
                                 preferred_element_type=F32)
            z = zh if z is None else z + zh
        mask = jnp.logical_and(kpos0 + lane < qpos, lane < n_valid)
        a, carry = _sb_tile(z * SCALE, u, carry_ref[...], mask)
        for h in range(SB_HEADS):
            acc_ref[h] += jnp.dot(a, get_v(h).astype(BF16), preferred_element_type=F32)
        carry_ref[...] = carry
        return (jnp.max(carry[:SB_HEADS]) < SB_DEAD_LOG).astype(jnp.int32)

    def new_rows(r):
        return jnp.where(prow < n_new, jnp.broadcast_to(new_ref[r:r + 1, :], (PAGE_SIZE, HEAD_DIM)), 0.0)

    page_copy(0, 0).start()
    done0 = process(new_rows, lambda h: new_rows(SB_HEADS + h), past, n_new)

    def cond(state):
        j, done = state
        return jnp.logical_and(j < n_pages, done == 0)

    def body(state):
        j, _ = state
        slot = j % 2
        page_copy(j, slot).wait()

        @pl.when(j + 1 < n_pages)
        def _():
            page_copy(j + 1, 1 - slot).start()

        done = process(lambda h: buf_ref[slot, pl.ds(h, PAGE_SIZE, stride=SB_CACHE_SLOTS), :],
                       lambda h: buf_ref[slot, pl.ds(SB_HEADS + h, PAGE_SIZE, stride=SB_CACHE_SLOTS), :],
                       (n_pages - 1 - j) * PAGE_SIZE, PAGE_SIZE)
        return j + 1, done

    j_end, _ = lax.while_loop(cond, body, (jnp.int32(0), done0))

    @pl.when(j_end < n_pages)
    def _():
        page_copy(j_end, j_end % 2).wait()

    row8 = lax.broadcasted_iota(jnp.int32, (SB_HEADS, HEAD_DIM), 0)
    out = jnp.zeros((SB_HEADS, HEAD_DIM), F32)
    for h in range(SB_HEADS):
        out = jnp.where(row8 == h, acc_ref[h][:SB_HEADS], out)
    o_ref[...] = out


def _sb_sample(q, kv_new, cache, page_table):
    b, n_pages = page_table.shape
    page_rows = PAGE_SIZE * SB_CACHE_SLOTS
    grid_spec = pltpu.PrefetchScalarGridSpec(
        num_scalar_prefetch=1,
        grid=(b,),
        in_specs=[
            pl.BlockSpec((None, SB_HEADS, HEAD_DIM), lambda bi, pt: (bi, 0, 0)),
            pl.BlockSpec((None, SB_CACHE_SLOTS, HEAD_DIM), lambda bi, pt: (bi, 0, 0)),
            pl.BlockSpec(memory_space=pl.ANY),
            pl.BlockSpec((PAGE_SIZE, PAGE_SIZE), lambda bi, pt: (0, 0)),
        ],
        out_specs=pl.BlockSpec((None, SB_HEADS, HEAD_DIM), lambda bi, pt: (bi, 0, 0)),
        scratch_shapes=[
            pltpu.VMEM((2, page_rows, HEAD_DIM), F32),
            pltpu.SemaphoreType.DMA((2,)),
            pltpu.VMEM((SB_HEADS, ROWS_PAD, HEAD_DIM), F32),
            pltpu.VMEM((ROWS_PAD, 1), F32),
        ],
    )
    return pl.pallas_call(
        functools.partial(_sb_sample_kernel, n_pages=n_pages, past=n_pages * PAGE_SIZE, n_new=1),
        grid_spec=grid_spec,
        out_shape=jax.ShapeDtypeStruct((b, SB_HEADS, HEAD_DIM), F32),
        compiler_params=_cparams(("arbitrary",)),
        name="sb_sample",
    )(page_table, q, kv_new, cache, _later_matrix(PAGE_SIZE))


def _nsa_main_sample_kernel(*refs, n_top, past, n_new):
    kh = NSA_KV_HEADS
    idx_ref, pt_ref, q_ref = refs[:3]
    blk_refs = refs[3:3 + kh * n_top]
    new_ref, win_ref, wnew_ref, ocmp_ref, gate_ref, o_ref = refs[3 + kh * n_top:]
    bi = pl.program_id(0)
    qpos = past + n_new - 1
    win_buf = win_ref.shape[0] // WIN_CACHE_SLOTS
    n_keys = n_top * SEL_BLOCK
    lane = lax.broadcasted_iota(jnp.int32, (ROWS_PAD, n_keys), 1)
    seg = lane // SEL_BLOCK

    def new_tile(ref, r, rows):
        row = lax.broadcasted_iota(jnp.int32, (rows, HEAD_DIM), 0)
        return jnp.where(row < n_new, jnp.broadcast_to(ref[r:r + 1, :], (rows, HEAD_DIM)), 0.0)

    def attend(q, k, v, mask):
        s = lax.dot_general(q, k.astype(BF16), (((1,), (1,)), ((), ())), preferred_element_type=F32) * SCALE
        return _masked_softmax_pv(s, mask, v.astype(BF16))

    for g in range(kh):
        q = q_ref[g]
        ks, vs = [], []
        base = jnp.zeros((ROWS_PAD, n_keys), jnp.int32)
        found = jnp.zeros((ROWS_PAD, n_keys), jnp.int32)
        for n in range(n_top):
            blk = idx_ref[(bi * kh + g) * n_top + n]
            is_new = blk * SEL_BLOCK >= past
            ref = blk_refs[g * n_top + n]
            kc = ref[pl.ds(2 * kh + g, SEL_BLOCK, stride=NSA_CACHE_SLOTS), :]
            vc = ref[pl.ds(3 * kh + g, SEL_BLOCK, stride=NSA_CACHE_SLOTS), :]
            ks.append(jnp.where(is_new, new_tile(new_ref, 2 * kh + g, SEL_BLOCK), kc))
            vs.append(jnp.where(is_new, new_tile(new_ref, 3 * kh + g, SEL_BLOCK), vc))
            base = jnp.where(seg == n, blk * SEL_BLOCK, base)
            found = jnp.where(seg == n, (blk >= 0).astype(jnp.int32), found)
        kpos = base + lane % SEL_BLOCK
        mask = jnp.logical_and(found > 0, jnp.logical_and(kpos <= qpos, kpos < past + n_new))
        o_sel = attend(q, jnp.concatenate(ks, axis=0), jnp.concatenate(vs, axis=0), mask)

        wk = jnp.concatenate([win_ref[pl.ds(g, win_buf, stride=WIN_CACHE_SLOTS), :],
                              new_tile(wnew_ref, g, LANES)], axis=0)
        wv = jnp.concatenate([win_ref[pl.ds(kh + g, win_buf, stride=WIN_CACHE_SLOTS), :],
                              new_tile(wnew_ref, kh + g, LANES)], axis=0)
        wlane = lax.broadcasted_iota(jnp.int32, (ROWS_PAD, win_buf + LANES), 1)
        wpos = past - win_buf + wlane
        diff = qpos - wpos
        wmask = jnp.logical_and(jnp.logical_and(diff >= 0, diff <= WINDOW),
                                jnp.logical_and(wpos >= 0, wpos < past + n_new))
        o_win = attend(q, wk, wv, wmask)
        gates = gate_ref[g]
        o_ref[g] = gates[:, 0:1] * ocmp_ref[g] + gates[:, 1:2] * o_sel + gates[:, 2:3] * o_win


def _nsa_main_sample(idx, page_table, q_rot, cache, rows_new, win_state, win_new, o_cmp, gates, *, n_top):
    b, n_pages = page_table.shape
    past = n_pages * PAGE_SIZE
    per_page = PAGE_SIZE // SEL_BLOCK
    kh = NSA_KV_HEADS

    def sel_spec(g, n):
        def index(bi, idx_ref, pt):
            blk = jnp.maximum(idx_ref[(bi * kh + g) * n_top + n], 0)
            page = jnp.minimum(blk // per_page, n_pages - 1)
            return pt[bi, page], blk % per_page, 0
        return pl.BlockSpec((None, SEL_BLOCK * NSA_CACHE_SLOTS, HEAD_DIM), index)

    def per_seq(rows):
        return pl.BlockSpec((None, rows, HEAD_DIM), lambda bi, idx_ref, pt: (bi, 0, 0))

    head_spec = pl.BlockSpec((None, kh, ROWS_PAD, HEAD_DIM), lambda bi, idx_ref, pt: (bi, 0, 0, 0))
    grid_spec = pltpu.PrefetchScalarGridSpec(
        num_scalar_prefetch=2,
        grid=(b,),
        in_specs=[head_spec] + [sel_spec(g, n) for g in range(kh) for n in range(n_top)] + [
            per_seq(NSA_CACHE_SLOTS), per_seq(win_state.shape[1]), per_seq(WIN_CACHE_SLOTS), head_spec, head_spec],
        out_specs=head_spec,
    )
    return pl.pallas_call(
        functools.partial(_nsa_main_sample_kernel, n_top=n_top, past=past, n_new=1),
        grid_spec=grid_spec,
        out_shape=jax.ShapeDtypeStruct((b, kh, ROWS_PAD, HEAD_DIM), F32),
        compiler_params=_cparams(("parallel",)),
        name="nsa_main_sample",
    )(idx, page_table, q_rot, *([cache] * (kh * n_top)), rows_new, win_state, win_new, o_cmp, gates)


def _merge_kernel(osb_ref, onsa_ref, g0_ref, g1_ref, pa_ref, pb_ref, wo_ref, x_ref, out_ref, acc_ref):
    j = pl.program_id(1)

    @pl.when(j == 0)
    def _():
        acc_ref[...] = jnp.zeros_like(acc_ref)

    a = jnp.dot(osb_ref[...], pa_ref[...], preferred_element_type=F32)
    b = jnp.dot(onsa_ref[...], pb_ref[...], preferred_element_type=F32)
    m = (g0_ref[...] * a + g1_ref[...] * b).astype(BF16)
    acc_ref[...] += jnp.dot(m, wo_ref[...], preferred_element_type=F32)

    @pl.when(j == pl.num_programs(1) - 1)
    def _():
        out_ref[...] = x_ref[...] + acc_ref[...]


def _merge(o_sb, o_nsa, mg, pa, pb, wo, x, *, tm_pref=512, tn=1024):
    m, d = x.shape
    ka, kb = o_sb.shape[1], o_nsa.shape[1]
    tm = _pick_tile(m, tm_pref)
    tn = _pick_tile(d, tn)
    nj = d // tn
    return pl.pallas_call(
        _merge_kernel,
        grid=(m // tm, nj),
        in_specs=[
            pl.BlockSpec((tm, ka), lambda i, j: (i, 0)),
            pl.BlockSpec((tm, kb), lambda i, j: (i, 0)),
            pl.BlockSpec((tm, tn), lambda i, j: (i, j)),
            pl.BlockSpec((tm, tn), lambda i, j: (i, nj + j)),
            pl.BlockSpec((ka, tn), lambda i, j: (0, j)),
            pl.BlockSpec((kb, tn), lambda i, j: (0, j)),
            pl.BlockSpec((tn, d), lambda i, j: (j, 0)),
            pl.BlockSpec((tm, d), lambda i, j: (i, 0)),
        ],
        out_specs=pl.BlockSpec((tm, d), lambda i, j: (i, 0)),
        out_shape=jax.ShapeDtypeStruct((m, d), F32),
        scratch_shapes=[pltpu.VMEM((tm, d), F32)],
        compiler_params=_cparams(("parallel", "arbitrary")),
        name="merge",
    )(o_sb, o_nsa, mg, mg, pa, pb, wo, x)


SB_Q_COLS = SB_HEADS * HEAD_DIM
SB_KV_COLS = 2 * SB_HEADS * HEAD_DIM
NSA_Q_COLS = NSA_Q_HEADS * HEAD_DIM
NSA_ROW_COLS = 4 * NSA_KV_HEADS * HEAD_DIM
NSA_WIN_COLS = 2 * NSA_KV_HEADS * HEAD_DIM
NSA_GATE_COLS = 3 * NSA_Q_HEADS


def _prep_layer_weights(layer, g_ffn1, ffn1_w_gu, ffn1_w_down, g_mix, w_in, cmp_pe_k, cmp_w_k, cmp_pe_v, cmp_w_v,
                        p_a, p_b, w_o, g_ffn2, ffn2_w_gu, ffn2_w_down):
    w = {}
    w["g1"], w["g_mix"], w["g2"] = g_ffn1[layer][None], g_mix[layer][None], g_ffn2[layer][None]
    w["ffn1"] = _prep_ffn_weights(ffn1_w_gu[layer], ffn1_w_down[layer])
    w["ffn2"] = _prep_ffn_weights(ffn2_w_gu[layer], ffn2_w_down[layer])
    wi = w_in[layer]
    off = 0
    for name, n in (("sbq", SB_Q_COLS), ("sbkv", SB_KV_COLS), ("nq", NSA_Q_COLS), ("rows", NSA_ROW_COLS),
                    ("win", NSA_WIN_COLS)):
        w[name] = wi[:, off:off + n].astype(BF16)
        off += n
    wg = wi[:, off:off + NSA_GATE_COLS].reshape(-1, 3, NSA_KV_HEADS, NSA_GROUP).transpose(0, 2, 1, 3)
    wg = wg.reshape(-1, NSA_KV_HEADS, 3 * NSA_GROUP)
    wg = jnp.pad(wg, ((0, 0), (0, 0), (0, LANES - 3 * NSA_GROUP)))
    w["gate"] = wg.reshape(-1, NSA_KV_HEADS * LANES).astype(BF16)
    off += NSA_GATE_COLS
    w["mg"] = wi[:, off:].astype(BF16)
    w["pe"] = jnp.stack([cmp_pe_k[layer], cmp_pe_v[layer]])
    w["wc"] = jnp.stack([cmp_w_k[layer], cmp_w_v[layer]]).astype(BF16)
    w["pa"], w["pb"], w["wo"] = p_a[layer].astype(BF16), p_b[layer].astype(BF16), w_o[layer].astype(BF16)
    return w


def _mixer_inputs(h, w, cos2, sin2):
    plain_bf = ("plain", None, BF16)
    wide = 4 * LANES
    (sbq,) = _proj(h, w["sbq"], (plain_bf,), tn=wide)
    kv_f, kv_b = _proj(h, w["sbkv"], (("plain", None, F32), plain_bf), tn=wide)
    nq_raw, nq_rot = _proj(h, w["nq"], (plain_bf, ("rope", None, BF16)), cos2, sin2, tn=wide)
    rows_f, rows_b = _proj(h, w["rows"], (("rope", (2,), F32), ("rope", (2,), BF16)), cos2, sin2)
    win_f, win_b = _proj(h, w["win"], (("rope", (0,), F32), ("rope", (0,), BF16)), cos2, sin2)
    (gates,) = _proj(h, w["gate"], (("sigmoid", None, F32),), tn=LANES)
    (mg,) = _proj(h, w["mg"], (("sigmoid", None, F32),), tn=wide)
    return dict(sbq=sbq, kv_f=kv_f, kv_b=kv_b, nq_raw=nq_raw, nq_rot=nq_rot, rows_f=rows_f, rows_b=rows_b,
                win_f=win_f, win_b=win_b, gates=gates, mg=mg)


def _prompt_layer(x, w, cos2, sin2, g_next, last):
    t = x.shape[0]
    assert t % LANES == 0 and t >= CMP_BLOCK
    x1, h = _ffn(x, w["g1"], *w["ffn1"], w["g_mix"], emit_x=True, norm_dtype=BF16)
    mi = _mixer_inputs(h, w, cos2, sin2)
    o_sb = _sb_prompt(mi["sbq"], mi["kv_b"])
    p0, p1 = _compress_prompt(mi["rows_f"], w["pe"], w["wc"])
    n_cmp = (t - CMP_BLOCK) // CMP_STRIDE + 1
    n_sel = -(-t // SEL_BLOCK)
    tq = _pick_tile(t, 128)
    o_cmp, sel_t, _ = _nsa_cmp(mi["nq_raw"][None], p0, p1, tq=tq, qpos0=0, n_cmp=n_cmp, n_sel=n_sel)
    sv_cols = mi["rows_b"][:, 3 * NSA_KV_HEADS * HEAD_DIM:]
    sv_t = sv_cols.reshape(t, NSA_KV_HEADS, HEAD_DIM).transpose(1, 2, 0)
    o_nsa = _nsa_main_prompt(mi["nq_rot"], mi["win_b"], mi["rows_b"], sv_t, sel_t[0], o_cmp[0], mi["gates"])
    x2 = _merge(o_sb, o_nsa, mi["mg"], w["pa"], w["pb"], w["wo"], x1)
    if last:
        (y,) = _ffn(x2, w["g2"], *w["ffn2"], g_next, emit_x=False, norm_dtype=F32)
        x3 = None
    else:
        x3, y = _ffn(x2, w["g2"], *w["ffn2"], g_next, emit_x=True, norm_dtype=F32)
    return x3, y, mi["kv_f"], mi["rows_f"], mi["win_f"]


def _pad_rows(a, rows):
    return jnp.pad(a, ((0, 0), (0, 0), (0, rows - a.shape[2]), (0, 0)))


def _sample_layer(x, w, cos2, sin2, cache_sb, cache_nsa, win_state, page_table, g_next, last):
    b = x.shape[0]
    n_pages = page_table.shape[1]
    past = n_pages * PAGE_SIZE
    total = past + 1
    kh, grp = NSA_KV_HEADS, NSA_GROUP
    x1, h = _ffn(x, w["g1"], *w["ffn1"], w["g_mix"], emit_x=True, norm_dtype=BF16)
    mi = _mixer_inputs(h, w, cos2, sin2)
    o_sb = _sb_sample(mi["sbq"].astype(F32).reshape(b, SB_HEADS, HEAD_DIM),
                      mi["kv_f"].reshape(b, SB_CACHE_SLOTS, HEAD_DIM), cache_sb, page_table)
    o_sb = o_sb.reshape(b, SB_HEADS * HEAD_DIM)

    p0, p1 = _compress_sample(cache_nsa, page_table, w["pe"], w["wc"])
    n_cmp = (total - CMP_BLOCK) // CMP_STRIDE + 1
    n_sel = -(-total // SEL_BLOCK)
    n_top = min(SEL_TOPN, n_sel)
    q_raw = jnp.pad(mi["nq_raw"][:, None, :], ((0, 0), (0, LANES - 1), (0, 0)))
    o_cmp, _, idx = _nsa_cmp(q_raw, p0, p1, tq=LANES, qpos0=past, n_cmp=n_cmp, n_sel=n_sel)
    idx = idx[:, :, :n_top, 0].reshape(-1)
    q_rot = _pad_rows(mi["nq_rot"].reshape(b, kh, grp, HEAD_DIM), ROWS_PAD)
    o_cmp = _pad_rows(o_cmp[:, 0, :].reshape(b, kh, grp, HEAD_DIM), ROWS_PAD)
    gates = mi["gates"].reshape(b, kh, LANES)[:, :, :3 * grp].reshape(b, kh, 3, grp).transpose(0, 1, 3, 2)
    gates = jnp.pad(gates, ((0, 0), (0, 0), (0, ROWS_PAD - grp), (0, LANES - 3)))
    o_nsa = _nsa_main_sample(idx, page_table, q_rot, cache_nsa, mi["rows_f"].reshape(b, NSA_CACHE_SLOTS, HEAD_DIM),
                             win_state, mi["win_f"].reshape(b, WIN_CACHE_SLOTS, HEAD_DIM), o_cmp, gates, n_top=n_top)
    o_nsa = o_nsa[:, :, :grp, :].reshape(b, NSA_Q_HEADS * HEAD_DIM)

    x2 = _merge(o_sb.astype(BF16), o_nsa.astype(BF16), mi["mg"], w["pa"], w["pb"], w["wo"], x1)
    if last:
        (y,) = _ffn(x2, w["g2"], *w["ffn2"], g_next, emit_x=False, norm_dtype=F32)
        x3 = None
    else:
        x3, y = _ffn(x2, w["g2"], *w["ffn2"], g_next, emit_x=True, norm_dtype=F32)
    return x3, y, mi["kv_f"], mi["rows_f"], mi["win_f"]


def kernel(x_prompt, x_sample, cache_sb_kv, cache_nsa_kv, state_win_kv, page_table, g_ffn1, ffn1_w_gu, ffn1_w_down,
           g_mix, w_in, cmp_pe_k, cmp_w_k, cmp_pe_v, cmp_w_v, p_a, p_b, w_o, g_ffn2, ffn2_w_gu, ffn2_w_down, g_final):
    bp, t, d = x_prompt.shape
    db, ds, _ = x_sample.shape
    depth = g_ffn1.shape[0]
    n_pool = cache_sb_kv.shape[1]
    assert bp == 1 and ds == 1
    past = page_table.shape[1] * PAGE_SIZE
    win_buf = state_win_kv.shape[2]
    cos_p, sin_p = _rope_tables(jnp.arange(t))
    cos_s, sin_s = _rope_tables(jnp.full((db,), past, jnp.int32))
    g_fin = g_final[None]

    xp = x_prompt.reshape(t, d)
    xs = x_sample.reshape(db, d)
    sb_p, nsa_p, win_p, sb_s, nsa_s, win_s = [], [], [], [], [], []
    yp = ys = None
    for layer in range(depth):
        w = _prep_layer_weights(layer, g_ffn1, ffn1_w_gu, ffn1_w_down, g_mix, w_in, cmp_pe_k, cmp_w_k, cmp_pe_v,
                                cmp_w_v, p_a, p_b, w_o, g_ffn2, ffn2_w_gu, ffn2_w_down)
        last = layer == depth - 1
        xp, yp, kv_f, rows_f, win_f = _prompt_layer(xp, w, cos_p, sin_p, g_fin, last)
        sb_p.append(kv_f.reshape(1, t, 2, SB_HEADS, HEAD_DIM))
        nsa_p.append(rows_f.reshape(1, t, 4, NSA_KV_HEADS, HEAD_DIM))
        wp = min(WINDOW, t)
        win_p.append(win_f[t - wp:].reshape(1, wp, 2, NSA_KV_HEADS, HEAD_DIM))

        cache_sb = cache_sb_kv[layer].reshape(n_pool, PAGE_SIZE * SB_CACHE_SLOTS, HEAD_DIM)
        cache_nsa = cache_nsa_kv[layer].reshape(n_pool, PAGE_SIZE * NSA_CACHE_SLOTS, HEAD_DIM)
        win_state = state_win_kv[layer].reshape(db, win_buf * WIN_CACHE_SLOTS, HEAD_DIM)
        xs, ys, kv_f, rows_f, win_f = _sample_layer(xs, w, cos_s, sin_s, cache_sb, cache_nsa, win_state, page_table,
                                                    g_fin, last)
        sb_s.append(kv_f.reshape(db, 1, 2, SB_HEADS, HEAD_DIM))
        nsa_s.append(rows_f.reshape(db, 1, 4, NSA_KV_HEADS, HEAD_DIM))
        win_all = jnp.concatenate([state_win_kv[layer], win_f.reshape(db, 1, 2, NSA_KV_HEADS, HEAD_DIM)], axis=1)
        win_s.append(win_all[:, win_all.shape[1] - win_buf:])
    return (yp.reshape(1, t, d), ys.reshape(db, 1, d), jnp.stack(sb_p), jnp.stack(nsa_p), jnp.stack(win_p),
            jnp.stack(sb_s), jnp.stack(nsa_s), jnp.stack(win_s))
```

```python
import functools

import numpy as np
import jax
import jax.numpy as jnp
from jax import lax
from jax.experimental import pallas as pl
from jax.experimental.pallas import tpu as pltpu

HEAD_DIM = 128
SB_HEADS = 8
NSA_Q_HEADS = 8
NSA_KV_HEADS = 2
NSA_GROUP = NSA_Q_HEADS // NSA_KV_HEADS
CMP_BLOCK = 32
CMP_STRIDE = 16
SEL_BLOCK = 64
SEL_TOPN = 16
WINDOW = 512
PAGE_SIZE = 128
ROPE_THETA = 10000.0
NORM_EPS = 1e-6
HALF_STEP = 0.5
SCALE = HEAD_DIM ** -0.5
NEG_INF = -1e30
FORCED_SCORE = 1e4
TINY = 1e-30

SB_DEAD_LOG = -104.0

EXP2_SCALE = SCALE * float(np.log2(np.e))
MAX_FLOOR = 0.1 * NEG_INF

LANES = 128
SB_CACHE_SLOTS = 2 * SB_HEADS
NSA_CACHE_SLOTS = 4 * NSA_KV_HEADS
WIN_CACHE_SLOTS = 2 * NSA_KV_HEADS
VMEM_LIMIT = 56 * 1024 * 1024

F32 = jnp.float32
BF16 = jnp.bfloat16


def _cparams(sem):
    return pltpu.CompilerParams(dimension_semantics=sem, vmem_limit_bytes=VMEM_LIMIT)


def _round_up(x, m):
    return -(-x // m) * m


def _pick_tile(n, pref):
    t = min(pref, n)
    while n % t:
        t //= 2
    return t


def _rms(x, g):
    return x * lax.rsqrt(jnp.mean(x * x, axis=-1, keepdims=True) + NORM_EPS) * g


def _sigmoid(x):
    return 1.0 / (1.0 + jnp.exp(-x))


def _ffn_kernel(x_ref, g_ref, wg_ref, wu_ref, wd_ref, g2_ref, *refs, emit_x):
    if emit_x:
        out_ref, n_ref, h_scr, acc_scr = refs
    else:
        n_ref, h_scr, acc_scr = refs
    j = pl.program_id(1)

    @pl.when(j == 0)
    def _():
        h_scr[...] = _rms(x_ref[...], g_ref[...]).astype(BF16)
        acc_scr[...] = jnp.zeros_like(acc_scr)

    h = h_scr[...]
    gate = jnp.dot(h, wg_ref[...], preferred_element_type=F32)
    up = jnp.dot(h, wu_ref[...], preferred_element_type=F32)
    act = (gate * _sigmoid(gate) * up).astype(BF16)
    acc_scr[...] += jnp.dot(act, wd_ref[...], preferred_element_type=F32)

    @pl.when(j == pl.num_programs(1) - 1)
    def _():
        y = x_ref[...] + HALF_STEP * acc_scr[...]
        if emit_x:
            out_ref[...] = y
        n_ref[...] = _rms(y, g2_ref[...]).astype(n_ref.dtype)


def _ffn(x, g, wg, wu, wd, g2, *, emit_x, norm_dtype, tm_pref=512, tf=512):
    m, d = x.shape
    fpad = wg.shape[1]
    tm = _pick_tile(m, tm_pref)
    grid = (m // tm, fpad // tf)
    out_shape = []
    out_specs = []
    if emit_x:
        out_shape.append(jax.ShapeDtypeStruct((m, d), F32))
        out_specs.append(pl.BlockSpec((tm, d), lambda i, j: (i, 0)))
    out_shape.append(jax.ShapeDtypeStruct((m, d), norm_dtype))
    out_specs.append(pl.BlockSpec((tm, d), lambda i, j: (i, 0)))
    return pl.pallas_call(
        functools.partial(_ffn_kernel, emit_x=emit_x),
        grid=grid,
        in_specs=[
            pl.BlockSpec((tm, d), lambda i, j: (i, 0)),
            pl.BlockSpec((1, d), lambda i, j: (0, 0)),
            pl.BlockSpec((d, tf), lambda i, j: (0, j)),
            pl.BlockSpec((d, tf), lambda i, j: (0, j)),
            pl.BlockSpec((tf, d), lambda i, j: (j, 0)),
            pl.BlockSpec((1, d), lambda i, j: (0, 0)),
        ],
        out_specs=out_specs,
        out_shape=out_shape,
        scratch_shapes=[pltpu.VMEM((tm, d), BF16), pltpu.VMEM((tm, d), F32)],
        compiler_params=_cparams(("parallel", "arbitrary")),
        name="ffn",
    )(x, g, wg, wu, wd, g2)


def _prep_ffn_weights(w_gu, w_down, tf=512):
    d, two_f = w_gu.shape
    f = two_f // 2
    fpad = _round_up(f, tf)
    wg = jnp.pad(w_gu[:, :f].astype(BF16), ((0, 0), (0, fpad - f)))
    wu = jnp.pad(w_gu[:, f:].astype(BF16), ((0, 0), (0, fpad - f)))
    wd = jnp.pad(w_down.astype(BF16), ((0, fpad - f), (0, 0)))
    return wg, wu, wd


def _rope_tile(x, cos2, sin2):
    return x * cos2 + pltpu.roll(x, HEAD_DIM // 2, axis=1) * sin2


def _proj_kernel(a_ref, w_ref, *refs, outs, use_rope):
    if use_rope:
        cos_ref, sin_ref = refs[:2]
        refs = refs[2:]
    j = pl.program_id(1)
    acc = jnp.dot(a_ref[...], w_ref[...], preferred_element_type=F32)
    tn = acc.shape[1]
    for (mode, rope_tiles, _), o_ref in zip(outs, refs):
        if mode == "sigmoid":
            o_ref[...] = _sigmoid(acc).astype(o_ref.dtype)
        elif mode == "rope":
            cos2 = cos_ref[...]
            sin2 = sin_ref[...]
            rot = jnp.concatenate(
                [_rope_tile(acc[:, c * LANES:(c + 1) * LANES], cos2, sin2) for c in range(tn // LANES)], axis=1)
            if rope_tiles is None:
                o_ref[...] = rot.astype(o_ref.dtype)
            else:
                is_rope = functools.reduce(jnp.logical_or, [j == t for t in rope_tiles])
                o_ref[...] = jnp.where(is_rope, rot, acc).astype(o_ref.dtype)
        else:
            o_ref[...] = acc.astype(o_ref.dtype)


def _proj(a, w, outs, cos2=None, sin2=None, *, tm_pref=1024, tn=256):
    m, k = a.shape
    n = w.shape[1]
    tm = _pick_tile(m, tm_pref)
    tn = min(tn, n)
    use_rope = any(mode == "rope" for mode, _, _ in outs)
    in_specs = [pl.BlockSpec((tm, k), lambda i, j: (i, 0)), pl.BlockSpec((k, tn), lambda i, j: (0, j))]
    args = [a, w]
    if use_rope:
        in_specs += [pl.BlockSpec((tm, LANES), lambda i, j: (i, 0))] * 2
        args += [cos2, sin2]
    res = pl.pallas_call(
        functools.partial(_proj_kernel, outs=outs, use_rope=use_rope),
        grid=(m // tm, n // tn),
        in_specs=in_specs,
        out_specs=[pl.BlockSpec((tm, tn), lambda i, j: (i, j)) for _ in outs],
        out_shape=[jax.ShapeDtypeStruct((m, n), dt) for _, _, dt in outs],
        compiler_params=_cparams(("parallel", "arbitrary")),
        name="proj",
    )(*args)
    return res


def _rope_tables(pos):
    half = HEAD_DIM // 2
    inv_freq = ROPE_THETA ** (-2.0 * jnp.arange(half, dtype=F32) / HEAD_DIM)
    ang = pos.astype(F32)[:, None] * inv_freq[None, :]
    cos, sin = jnp.cos(ang), jnp.sin(ang)
    return jnp.concatenate([cos, cos], axis=1), jnp.concatenate([-sin, sin], axis=1)


def _later_matrix(tk):
    j = np.arange(tk)[:, None]
    s = np.arange(tk)[None, :]
    return jnp.asarray((j > s).astype(np.float32), dtype=BF16)


def _sb_tile(z, u, carry, mask):
    soft = jnp.log(1.0 + jnp.exp(-jnp.abs(z)))
    log_beta = jnp.minimum(z, 0.0) - soft
    log_keep = jnp.where(mask, jnp.minimum(-z, 0.0) - soft, 0.0)
    hi = log_keep.astype(BF16)
    lo = (log_keep - hi.astype(F32)).astype(BF16)
    later = (jnp.dot(hi, u, preferred_element_type=F32) + jnp.dot(lo, u, preferred_element_type=F32)) + carry
    a = jnp.where(mask, jnp.exp(log_beta + later), 0.0)
    return a.astype(BF16), carry + jnp.sum(log_keep, axis=1, keepdims=True)


def _sb_prompt_kernel(q_ref, k_ref, v_ref, u_ref, o_ref, acc_ref, carry_ref, *, tq, tk):
    qi = pl.program_id(1)
    acc_ref[...] = jnp.zeros_like(acc_ref)
    carry_ref[...] = jnp.zeros_like(carry_ref)
    q = q_ref[...]
    u = u_ref[...]
    qpos = qi * tq + lax.broadcasted_iota(jnp.int32, (tq, tk), 0)
    lane = lax.broadcasted_iota(jnp.int32, (tq, tk), 1)

    end = (qi + 1) * tq

    def cond(state):
        n, done = state
        return jnp.logical_and(end - n * tk > 0, done == 0)

    def body(state):
        n, _ = state
        hi = end - n * tk
        start = pl.multiple_of(jnp.maximum(hi - tk, 0), tq)
        k = k_ref[pl.ds(start, tk), :]
        v = v_ref[pl.ds(start, tk), :]
        z = lax.dot_general(q, k, (((1,), (1,)), ((), ())), preferred_element_type=F32) * SCALE
        kpos = start + lane
        mask = jnp.logical_and(kpos < qpos, kpos < hi)
        a, carry = _sb_tile(z, u, carry_ref[...], mask)
        acc_ref[...] += jnp.dot(a, v, preferred_element_type=F32)
        carry_ref[...] = carry
        done = (jnp.max(carry) < SB_DEAD_LOG).astype(jnp.int32)
        return n + 1, done

    lax.while_loop(cond, body, (jnp.int32(0), jnp.int32(0)))
    o_ref[...] = acc_ref[...].astype(o_ref.dtype)


def _sb_prompt(q_bf, kv_bf, *, tq=256, tk=512):
    t = q_bf.shape[0]
    tq = _pick_tile(t, tq)
    tk = _pick_tile(t, tk)
    assert tk % tq == 0
    return pl.pallas_call(
        functools.partial(_sb_prompt_kernel, tq=tq, tk=tk),
        grid=(SB_HEADS, t // tq),
        in_specs=[
            pl.BlockSpec((tq, HEAD_DIM), lambda h, i: (i, h)),
            pl.BlockSpec((t, HEAD_DIM), lambda h, i: (0, h)),
            pl.BlockSpec((t, HEAD_DIM), lambda h, i: (0, SB_HEADS + h)),
            pl.BlockSpec((tk, tk), lambda h, i: (0, 0)),
        ],
        out_specs=pl.BlockSpec((tq, HEAD_DIM), lambda h, i: (i, h)),
        out_shape=jax.ShapeDtypeStruct((t, SB_HEADS * HEAD_DIM), BF16),
        scratch_shapes=[pltpu.VMEM((tq, HEAD_DIM), F32), pltpu.VMEM((tq, 1), F32)],
        compiler_params=_cparams(("parallel", "arbitrary")),
        name="sb_prompt",
    )(q_bf, kv_bf, kv_bf, _later_matrix(tk))


def _compress_kernel(*refs, n_in, rows, prefetch):
    if prefetch:
        refs = refs[1:]
    in_refs = refs[:n_in]
    pe_ref, w_ref, p0_ref, p1_ref = refs[n_in:n_in + 4]
    col_refs = refs[n_in + 4:]
    nch = rows // CMP_STRIDE
    for sg in range(2 * NSA_KV_HEADS):
        slot = sg // NSA_KV_HEADS
        col_ref = col_refs[sg]
        for i, r in enumerate(in_refs):
            if prefetch:
                col_ref[i * rows:(i + 1) * rows, :] = r[pl.ds(sg, rows, stride=NSA_CACHE_SLOTS), :]
            else:
                col_ref[i * rows:(i + 1) * rows, :] = r[:, sg * LANES:(sg + 1) * LANES]
        acc0 = jnp.zeros((n_in * nch, HEAD_DIM), F32)
        acc1 = jnp.zeros((n_in * nch, HEAD_DIM), F32)
        for j in range(CMP_STRIDE):
            x = col_ref[pl.ds(j, n_in * nch, stride=CMP_STRIDE), :]
            x0 = (x + pe_ref[slot, pl.ds(j, 1), :]).astype(BF16)
            x1 = (x + pe_ref[slot, pl.ds(CMP_STRIDE + j, 1), :]).astype(BF16)
            acc0 += jnp.dot(x0, w_ref[slot, j], preferred_element_type=F32)
            acc1 += jnp.dot(x1, w_ref[slot, CMP_STRIDE + j], preferred_element_type=F32)
        p0_ref[sg] = acc0
        p1_ref[sg] = acc1


def _compress_prompt(rows_f32, pe, w, *, rows_pref=2048):
    t = rows_f32.shape[0]
    rows = _pick_tile(t, rows_pref)
    nch = rows // CMP_STRIDE
    n_sg = 2 * NSA_KV_HEADS
    out_sds = jax.ShapeDtypeStruct((1, n_sg, t // CMP_STRIDE, HEAD_DIM), F32)
    out_spec = pl.BlockSpec((None, n_sg, nch, HEAD_DIM), lambda i: (0, 0, i, 0))
    return pl.pallas_call(
        functools.partial(_compress_kernel, n_in=1, rows=rows, prefetch=False),
        grid=(t // rows,),
        in_specs=[
            pl.BlockSpec((rows, n_sg * LANES), lambda i: (i, 0)),
            pl.BlockSpec(pe.shape, lambda i: (0, 0, 0)),
            pl.BlockSpec(w.shape, lambda i: (0, 0, 0, 0)),
        ],
        out_specs=[out_spec, out_spec],
        out_shape=[out_sds, out_sds],
        scratch_shapes=[pltpu.VMEM((rows, HEAD_DIM), F32)] * n_sg,
        compiler_params=_cparams(("parallel",)),
        name="compress_prompt",
    )(rows_f32, pe, w)


def _compress_sample(cache, page_table, pe, w, *, pages_per_step=32):
    b, n_pages = page_table.shape
    npp = _pick_tile(n_pages, pages_per_step)
    nch = PAGE_SIZE // CMP_STRIDE
    n_sg = 2 * NSA_KV_HEADS
    out_sds = jax.ShapeDtypeStruct((b, n_sg, n_pages * nch, HEAD_DIM), F32)
    out_spec = pl.BlockSpec((None, n_sg, npp * nch, HEAD_DIM), lambda bi, i, pt: (bi, 0, i, 0))

    def page_spec(k):
        return pl.BlockSpec((None, PAGE_SIZE * NSA_CACHE_SLOTS, HEAD_DIM),
                            lambda bi, i, pt: (pt[bi, i * npp + k], 0, 0))

    grid_spec = pltpu.PrefetchScalarGridSpec(
        num_scalar_prefetch=1,
        grid=(b, n_pages // npp),
        in_specs=[page_spec(k) for k in range(npp)] + [
            pl.BlockSpec(pe.shape, lambda bi, i, pt: (0, 0, 0)),
            pl.BlockSpec(w.shape, lambda bi, i, pt: (0, 0, 0, 0)),
        ],
        out_specs=[out_spec, out_spec],
        scratch_shapes=[pltpu.VMEM((npp * PAGE_SIZE, HEAD_DIM), F32)] * n_sg,
    )
    return pl.pallas_call(
        functools.partial(_compress_kernel, n_in=npp, rows=PAGE_SIZE, prefetch=True),
        grid_spec=grid_spec,
        out_shape=[out_sds, out_sds],
        compiler_params=_cparams(("parallel", "arbitrary")),
        name="compress_sample",
    )(page_table, *([cache] * npp), pe, w)


def _selection_matrix(n_c, n_cmp, n_sel, n_sel_pad):
    ratio = SEL_BLOCK // CMP_STRIDE
    lo = -((CMP_BLOCK - 1) // CMP_STRIDE)
    hi = (SEL_BLOCK - 1) // CMP_STRIDE
    c = np.arange(n_c)[None, :]
    b = np.arange(n_sel_pad)[:, None]
    hit = (c >= ratio * b + lo) & (c <= ratio * b + hi) & (c < n_cmp) & (b < n_sel)
    return jnp.asarray(hit.astype(np.float32), dtype=BF16)


def _split_dot(m01, x):
    hi = x.astype(BF16)
    lo = (x - hi.astype(F32)).astype(BF16)
    return jnp.dot(m01, hi, preferred_element_type=F32) + jnp.dot(m01, lo, preferred_element_type=F32)


def _nsa_cmp_kernel(q_ref, p0k_ref, p1k_ref, p0v_ref, p1v_ref, smap_ref, ocmp_ref, sel_ref, idx_ref, ck_ref, cvt_ref,
                    *, tq, qpos0, n_sel, n_top):
    qi = pl.program_id(2)
    n_c = p0k_ref.shape[0]
    n_sel_pad = smap_ref.shape[0]

    @pl.when(qi == 0)
    def _():
        ck_ref[...] = (p0k_ref[...] + pltpu.roll(p1k_ref[...], n_c - 1, axis=0)).astype(BF16)
        cvt_ref[...] = (p0v_ref[...] + pltpu.roll(p1v_ref[...], n_c - 1, axis=0)).T.astype(BF16)

    q = q_ref[...]
    ck = ck_ref[...]
    cvt = cvt_ref[...]
    qpos_c = qpos0 + qi * tq + lax.broadcasted_iota(jnp.int32, (n_c, tq), 1)
    cend = lax.broadcasted_iota(jnp.int32, (n_c, tq), 0) * CMP_STRIDE + (CMP_BLOCK - 1)
    bias = jnp.where(cend <= qpos_c, 0.0, NEG_INF)
    imp = None
    outs = []
    for r in range(NSA_GROUP):
        s = lax.dot_general(ck, q[:, r * LANES:(r + 1) * LANES], (((1,), (1,)), ((), ())),
                            preferred_element_type=F32) + bias
        m = jnp.maximum(jnp.max(s, axis=0, keepdims=True), MAX_FLOOR)
        p = jnp.exp2((s - m) * EXP2_SCALE)
        p = p * (1.0 / jnp.maximum(jnp.sum(p, axis=0, keepdims=True), TINY))
        outs.append(jnp.dot(cvt, p.astype(BF16), preferred_element_type=F32))
        imp = p if imp is None else imp + p
    o = jnp.concatenate(outs, axis=1).T
    ocmp_ref[...] = jnp.concatenate([o[r * tq:(r + 1) * tq] for r in range(NSA_GROUP)], axis=1)

    imp_sel = _split_dot(smap_ref[...], imp)
    qpos = qpos0 + qi * tq + lax.broadcasted_iota(jnp.int32, (n_sel_pad, tq), 1)
    blk = lax.broadcasted_iota(jnp.int32, (n_sel_pad, tq), 0)
    cur = qpos // SEL_BLOCK
    eligible = jnp.logical_and(blk * SEL_BLOCK <= qpos, blk < n_sel)
    forced = jnp.logical_or(blk == 0, jnp.logical_or(blk == cur, blk == cur - 1))
    score = jnp.where(eligible, jnp.where(forced, FORCED_SCORE, imp_sel), NEG_INF)
    blkf = blk.astype(F32)
    pick = lax.broadcasted_iota(jnp.int32, (idx_ref.shape[0], tq), 0)
    selected = jnp.zeros((n_sel_pad, tq), F32)
    idx_acc = jnp.full(pick.shape, -1.0, F32)
    for i in range(n_top):
        mx = jnp.max(score, axis=0, keepdims=True)
        first = jnp.min(jnp.where(score == mx, blkf, 1e9), axis=0, keepdims=True)
        valid = mx > 0.5 * NEG_INF
        hit = blkf == first
        selected = jnp.where(jnp.logical_and(hit, valid), 1.0, selected)
        idx_acc = jnp.where(pick == i, jnp.where(valid, first, -1.0), idx_acc)
        score = jnp.where(hit, -3e38, score)
    sel_ref[...] = selected.astype(sel_ref.dtype)
    idx_ref[...] = idx_acc.astype(jnp.int32)


def _nsa_cmp(q_bf, p0, p1, *, tq, qpos0, n_cmp, n_sel):
    b, t, _ = q_bf.shape
    n_c = p0.shape[2]
    n_sel_pad = _round_up(n_sel, LANES)
    n_top = min(SEL_TOPN, n_sel)
    pick_rows = _round_up(n_top, 8)
    smap = _selection_matrix(n_c, n_cmp, n_sel, n_sel_pad)
    gw = NSA_GROUP * HEAD_DIM

    def part_spec(slot):
        return pl.BlockSpec((None, None, n_c, HEAD_DIM), lambda bi, g, i: (bi, slot * NSA_KV_HEADS + g, 0, 0))

    return pl.pallas_call(
        functools.partial(_nsa_cmp_kernel, tq=tq, qpos0=qpos0, n_sel=n_sel, n_top=n_top),
        grid=(b, NSA_KV_HEADS, t // tq),
        in_specs=[
            pl.BlockSpec((None, tq, gw), lambda bi, g, i: (bi, i, g)),
            part_spec(0), part_spec(0), part_spec(1), part_spec(1),
            pl.BlockSpec((n_sel_pad, n_c), lambda bi, g, i: (0, 0)),
        ],
        out_specs=[
            pl.BlockSpec((None, tq, gw), lambda bi, g, i: (bi, i, g)),
            pl.BlockSpec((None, None, n_sel_pad, tq), lambda bi, g, i: (bi, g, 0, i)),
            pl.BlockSpec((None, None, pick_rows, tq), lambda bi, g, i: (bi, g, 0, i)),
        ],
        out_shape=[
            jax.ShapeDtypeStruct((b, t, NSA_Q_HEADS * HEAD_DIM), F32),
            jax.ShapeDtypeStruct((b, NSA_KV_HEADS, n_sel_pad, t), BF16),
            jax.ShapeDtypeStruct((b, NSA_KV_HEADS, pick_rows, t), jnp.int32),
        ],
        scratch_shapes=[pltpu.VMEM((n_c, HEAD_DIM), BF16), pltpu.VMEM((HEAD_DIM, n_c), BF16)],
        compiler_params=_cparams(("arbitrary", "arbitrary", "arbitrary")),
        name="nsa_cmp",
    )(q_bf, p0, p1, p0, p1, smap)


def _stack_heads(q):
    return jnp.concatenate([q[:, r * LANES:(r + 1) * LANES] for r in range(NSA_GROUP)], axis=0)


def _masked_softmax_pv(s, mask, v):
    s = jnp.where(mask, s, NEG_INF)
    p = jnp.where(mask, jnp.exp(s - jnp.max(s, axis=1, keepdims=True)), 0.0)
    p = p / jnp.maximum(jnp.sum(p, axis=1, keepdims=True), TINY)
    return jnp.dot(p.astype(BF16), v, preferred_element_type=F32)


def _combine_branches(gates, o_cmp, o_sel, o_win, tq):
    outs = []
    for r in range(NSA_GROUP):
        rows = slice(r * tq, (r + 1) * tq)
        outs.append(gates[:, r:r + 1] * o_cmp[:, r * LANES:(r + 1) * LANES]
                    + gates[:, NSA_GROUP + r:NSA_GROUP + r + 1] * o_sel[rows]
                    + gates[:, 2 * NSA_GROUP + r:2 * NSA_GROUP + r + 1] * o_win[rows])
    return jnp.concatenate(outs, axis=1)


def _nsa_main_kernel(*refs, tq, tk, n_wblk):
    q_ref = refs[0]
    wk_refs = refs[1:1 + n_wblk]
    wv_refs = refs[1 + n_wblk:1 + 2 * n_wblk]
    (sk_ref, svt_ref, sel_ref, ocmp_ref, gate_ref, o_ref, m_ref, l_ref, acc_ref,
     sa_ref, sb_ref, ba_ref, bb_ref) = refs[1 + 2 * n_wblk:]
    qi = pl.program_id(1)
    qs = _stack_heads(q_ref[...])
    n_sel_pad = sel_ref.shape[0]

    wk = jnp.concatenate([r[...] for r in wk_refs], axis=0)
    wvt = jnp.concatenate([r[...] for r in wv_refs], axis=1)
    wlen = n_wblk * tq
    qpos_w = qi * tq + lax.broadcasted_iota(jnp.int32, (wlen, tq), 1)
    wpos = (qi - (n_wblk - 1)) * tq + lax.broadcasted_iota(jnp.int32, (wlen, tq), 0)
    diff = qpos_w - wpos
    bias_w = jnp.where(jnp.logical_and(jnp.logical_and(diff >= 0, diff <= WINDOW), wpos >= 0), 0.0, NEG_INF)
    s_w = lax.dot_general(wk, qs, (((1,), (1,)), ((), ())), preferred_element_type=F32)
    win_parts = []
    for r in range(NSA_GROUP):
        s = s_w[:, r * tq:(r + 1) * tq] + bias_w
        m = jnp.maximum(jnp.max(s, axis=0, keepdims=True), MAX_FLOOR)
        p = jnp.exp2((s - m) * EXP2_SCALE)
        p = p * (1.0 / jnp.maximum(jnp.sum(p, axis=0, keepdims=True), TINY))
        win_parts.append(jnp.dot(wvt, p.astype(BF16), preferred_element_type=F32))
    o_win = jnp.concatenate(win_parts, axis=1).T

    m_ref[...] = jnp.full_like(m_ref, MAX_FLOOR)
    l_ref[...] = jnp.zeros_like(l_ref)
    acc_ref[...] = jnp.zeros_like(acc_ref)
    selt = sel_ref[...]
    blocks_per_tile = tk // SEL_BLOCK
    rel = (lax.broadcasted_iota(jnp.int32, (tk, n_sel_pad), 1)
           - lax.broadcasted_iota(jnp.int32, (tk, n_sel_pad), 0) // SEL_BLOCK)
    qpos = qi * tq + lax.broadcasted_iota(jnp.int32, (tk, tq), 1)
    krow = lax.broadcasted_iota(jnp.int32, (tk, tq), 0)

    t_keys = sk_ref.shape[0]

    def load_start(j):
        return pl.multiple_of(jnp.minimum(j * tk, t_keys - tk), tk)

    def tile_scores(j, s_buf, bias_buf):
        expand = jnp.where(rel == j * blocks_per_tile, 1.0, 0.0).astype(BF16)
        picked = jnp.dot(expand, selt, preferred_element_type=F32)
        bias_buf[...] = jnp.where(jnp.logical_and(picked > 0.5, j * tk + krow <= qpos), 0.0, NEG_INF)
        k = sk_ref[pl.ds(load_start(j), tk), :]
        s_buf[...] = lax.dot_general(k, qs, (((1,), (1,)), ((), ())), preferred_element_type=F32)

    def tile_update(j, s_buf, bias_buf):
        vt = svt_ref[:, pl.ds(load_start(j), tk)]
        bias = bias_buf[...]
        ps, alphas = [], []
        for r in range(NSA_GROUP):
            cols = slice(r * tq, (r + 1) * tq)
            s = s_buf[:, cols] + bias
            m_old = m_ref[:, cols]
            m_new = jnp.maximum(m_old, jnp.max(s, axis=0, keepdims=True))
            p = jnp.exp2((s - m_new) * EXP2_SCALE)
            alpha = jnp.exp2((m_old - m_new) * EXP2_SCALE)
            l_ref[:, cols] = alpha * l_ref[:, cols] + jnp.sum(p, axis=0, keepdims=True)
            m_ref[:, cols] = m_new
            ps.append(p.astype(BF16))
            alphas.append(alpha)
        pv = jnp.dot(vt, jnp.concatenate(ps, axis=1), preferred_element_type=F32)
        acc_ref[...] = jnp.concatenate(alphas, axis=1) * acc_ref[...] + pv

    def body(i, carry):
        j = 2 * i
        tile_scores(j + 1, sb_ref, bb_ref)
        tile_update(j, sa_ref, ba_ref)
        tile_scores(j + 2, sa_ref, ba_ref)
        tile_update(j + 1, sb_ref, bb_ref)
        return carry

    n_tiles = ((qi + 1) * tq + tk - 1) // tk
    tile_scores(0, sa_ref, ba_ref)
    lax.fori_loop(0, (n_tiles + 1) // 2, body, 0)
    o_sel = (acc_ref[...] / jnp.maximum(l_ref[...], TINY)).T
    o_ref[...] = _combine_branches(gate_ref[...], ocmp_ref[...], o_sel, o_win, tq).astype(o_ref.dtype)


def _nsa_main_prompt(q_rot_bf, win_bf, wv_t, rows_bf, sv_t, sel_t, o_cmp, gates, *, tq=128, tk=512):
    t = q_rot_bf.shape[0]
    tq = _pick_tile(t, tq)
    tk = _pick_tile(t, tk)
    assert WINDOW % tq == 0 and tk % SEL_BLOCK == 0
    n_wblk = WINDOW // tq + 1
    n_sel_pad = sel_t.shape[1]
    gw = NSA_GROUP * HEAD_DIM

    def win_spec(s, col0):
        return pl.BlockSpec((tq, HEAD_DIM), lambda g, i: (jnp.maximum(i - (n_wblk - 1) + s, 0), col0 + g))

    in_specs = [pl.BlockSpec((tq, gw), lambda g, i: (i, g))]
    def win_t_spec(s):
        return pl.BlockSpec((None, HEAD_DIM, tq), lambda g, i: (g, 0, jnp.maximum(i - (n_wblk - 1) + s, 0)))

    in_specs += [win_spec(s, 0) for s in range(n_wblk)]
    in_specs += [win_t_spec(s) for s in range(n_wblk)]
    in_specs += [
        pl.BlockSpec((t, HEAD_DIM), lambda g, i: (0, 2 * NSA_KV_HEADS + g)),
        pl.BlockSpec((None, HEAD_DIM, t), lambda g, i: (g, 0, 0)),
        pl.BlockSpec((None, n_sel_pad, tq), lambda g, i: (g, 0, i)),
        pl.BlockSpec((tq, gw), lambda g, i: (i, g)),
        pl.BlockSpec((tq, LANES), lambda g, i: (i, g)),
    ]
    return pl.pallas_call(
        functools.partial(_nsa_main_kernel, tq=tq, tk=tk, n_wblk=n_wblk),
        grid=(NSA_KV_HEADS, t // tq),
        in_specs=in_specs,
        out_specs=pl.BlockSpec((tq, gw), lambda g, i: (i, g)),
        out_shape=jax.ShapeDtypeStruct((t, NSA_Q_HEADS * HEAD_DIM), BF16),
        scratch_shapes=[
            pltpu.VMEM((1, NSA_GROUP * tq), F32),
            pltpu.VMEM((1, NSA_GROUP * tq), F32),
            pltpu.VMEM((HEAD_DIM, NSA_GROUP * tq), F32),
            pltpu.VMEM((tk, NSA_GROUP * tq), F32),
            pltpu.VMEM((tk, NSA_GROUP * tq), F32),
            pltpu.VMEM((tk, tq), F32),
            pltpu.VMEM((tk, tq), F32),
        ],
        compiler_params=_cparams(("parallel", "arbitrary")),
        name="nsa_main_prompt",
    )(q_rot_bf, *([win_bf] * n_wblk), *([wv_t] * n_wblk), rows_bf, sv_t, sel_t, o_cmp, gates)


ROWS_PAD = 16


def _sb_sample_kernel(pt_ref, q_ref, new_ref, cache_ref, u_ref, o_ref, buf_ref, sem, acc_ref, carry_ref,
                      *, n_pages, past, n_new):
    bi = pl.program_id(0)
    qpos = past + n_new - 1
    row = lax.broadcasted_iota(jnp.int32, (ROWS_PAD, HEAD_DIM), 0)
    lane = lax.broadcasted_iota(jnp.int32, (ROWS_PAD, PAGE_SIZE), 1)
    prow = lax.broadcasted_iota(jnp.int32, (PAGE_SIZE, HEAD_DIM), 0)
    q16 = jnp.concatenate([q_ref[...], jnp.zeros((ROWS_PAD - SB_HEADS, HEAD_DIM), F32)], axis=0)
    q_rows = [jnp.where(row == h, q16, 0.0).astype(BF16) for h in range(SB_HEADS)]
    acc_ref[...] = jnp.zeros_like(acc_ref)
    carry_ref[...] = jnp.zeros_like(carry_ref)
    u = u_ref[...]

    def page_copy(j, slot):
        return pltpu.make_async_copy(cache_ref.at[pt_ref[bi, n_pages - 1 - j]], buf_ref.at[slot], sem.at[slot])

    def process(get_k, get_v, kpos0, n_valid):
        z = None
        for h in range(SB_HEADS):
            zh = lax.dot_general(q_rows[h], get_k(h).astype(BF16), (((1,), (1,)), ((), ())),
                                 preferred_element_type=F32)
            z = zh if z is None else z + zh
        mask = jnp.logical_and(kpos0 + lane < qpos, lane < n_valid)
        a, carry = _sb_tile(z * SCALE, u, carry_ref[...], mask)
        for h in range(SB_HEADS):
            acc_ref[h] += jnp.dot(a, get_v(h).astype(BF16), preferred_element_type=F32)
        carry_ref[...] = carry
        return (jnp.max(carry[:SB_HEADS]) < SB_DEAD_LOG).astype(jnp.int32)

    def new_rows(r):
        return jnp.where(prow < n_new, jnp.broadcast_to(new_ref[r:r + 1, :], (PAGE_SIZE, HEAD_DIM)), 0.0)

    page_copy(0, 0).start()
    done0 = process(new_rows, lambda h: new_rows(SB_HEADS + h), past, n_new)

    def cond(state):
        j, done = state
        return jnp.logical_and(j < n_pages, done == 0)

    def body(state):
        j, _ = state
        slot = j % 2
        page_copy(j, slot).wait()

        @pl.when(j + 1 < n_pages)
        def _():
            page_copy(j + 1, 1 - slot).start()

        done = process(lambda h: buf_ref[slot, pl.ds(h, PAGE_SIZE, stride=SB_CACHE_SLOTS), :],
                       lambda h: buf_ref[slot, pl.ds(SB_HEADS + h, PAGE_SIZE, stride=SB_CACHE_SLOTS), :],
                       (n_pages - 1 - j) * PAGE_SIZE, PAGE_SIZE)
        return j + 1, done

    j_end, _ = lax.while_loop(cond, body, (jnp.int32(0), done0))

    @pl.when(j_end < n_pages)
    def _():
        page_copy(j_end, j_end % 2).wait()

    row8 = lax.broadcasted_iota(jnp.int32, (SB_HEADS, HEAD_DIM), 0)
    out = jnp.zeros((SB_HEADS, HEAD_DIM), F32)
    for h in range(SB_HEADS):
        out = jnp.where(row8 == h, acc_ref[h][:SB_HEADS], out)
    o_ref[...] = out


def _sb_sample(q, kv_new, cache, page_table):
    b, n_pages = page_table.shape
    page_rows = PAGE_SIZE * SB_CACHE_SLOTS
    grid_spec = pltpu.PrefetchScalarGridSpec(
        num_scalar_prefetch=1,
        grid=(b,),
        in_specs=[
            pl.BlockSpec((None, SB_HEADS, HEAD_DIM), lambda bi, pt: (bi, 0, 0)),
            pl.BlockSpec((None, SB_CACHE_SLOTS, HEAD_DIM), lambda bi, pt: (bi, 0, 0)),
            pl.BlockSpec(memory_space=pl.ANY),
            pl.BlockSpec((PAGE_SIZE, PAGE_SIZE), lambda bi, pt: (0, 0)),
        ],
        out_specs=pl.BlockSpec((None, SB_HEADS, HEAD_DIM), lambda bi, pt: (bi, 0, 0)),
        scratch_shapes=[
            pltpu.VMEM((2, page_rows, HEAD_DIM), F32),
            pltpu.SemaphoreType.DMA((2,)),
            pltpu.VMEM((SB_HEADS, ROWS_PAD, HEAD_DIM), F32),
            pltpu.VMEM((ROWS_PAD, 1), F32),
        ],
    )
    return pl.pallas_call(
        functools.partial(_sb_sample_kernel, n_pages=n_pages, past=n_pages * PAGE_SIZE, n_new=1),
        grid_spec=grid_spec,
        out_shape=jax.ShapeDtypeStruct((b, SB_HEADS, HEAD_DIM), F32),
        compiler_params=_cparams(("arbitrary",)),
        name="sb_sample",
    )(page_table, q, kv_new, cache, _later_matrix(PAGE_SIZE))


def _nsa_main_sample_kernel(*refs, n_top, past, n_new):
    kh = NSA_KV_HEADS
    idx_ref, pt_ref, q_ref = refs[:3]
    blk_refs = refs[3:3 + kh * n_top]
    new_ref, win_ref, wnew_ref, ocmp_ref, gate_ref, o_ref = refs[3 + kh * n_top:]
    bi = pl.program_id(0)
    qpos = past + n_new - 1
    win_buf = win_ref.shape[0] // WIN_CACHE_SLOTS
    n_keys = n_top * SEL_BLOCK
    lane = lax.broadcasted_iota(jnp.int32, (ROWS_PAD, n_keys), 1)
    seg = lane // SEL_BLOCK

    def new_tile(ref, r, rows):
        row = lax.broadcasted_iota(jnp.int32, (rows, HEAD_DIM), 0)
        return jnp.where(row < n_new, jnp.broadcast_to(ref[r:r + 1, :], (rows, HEAD_DIM)), 0.0)

    def attend(q, k, v, mask):
        s = lax.dot_general(q, k.astype(BF16), (((1,), (1,)), ((), ())), preferred_element_type=F32) * SCALE
        return _masked_softmax_pv(s, mask, v.astype(BF16))

    for g in range(kh):
        q = q_ref[g]
        ks, vs = [], []
        base = jnp.zeros((ROWS_PAD, n_keys), jnp.int32)
        found = jnp.zeros((ROWS_PAD, n_keys), jnp.int32)
        for n in range(n_top):
            blk = idx_ref[(bi * kh + g) * n_top + n]
            is_new = blk * SEL_BLOCK >= past
            ref = blk_refs[g * n_top + n]
            kc = ref[pl.ds(2 * kh + g, SEL_BLOCK, stride=NSA_CACHE_SLOTS), :]
            vc = ref[pl.ds(3 * kh + g, SEL_BLOCK, stride=NSA_CACHE_SLOTS), :]
            ks.append(jnp.where(is_new, new_tile(new_ref, 2 * kh + g, SEL_BLOCK), kc))
            vs.append(jnp.where(is_new, new_tile(new_ref, 3 * kh + g, SEL_BLOCK), vc))
            base = jnp.where(seg == n, blk * SEL_BLOCK, base)
            found = jnp.where(seg == n, (blk >= 0).astype(jnp.int32), found)
        kpos = base + lane % SEL_BLOCK
        mask = jnp.logical_and(found > 0, jnp.logical_and(kpos <= qpos, kpos < past + n_new))
        o_sel = attend(q, jnp.concatenate(ks, axis=0), jnp.concatenate(vs, axis=0), mask)

        wk = jnp.concatenate([win_ref[pl.ds(g, win_buf, stride=WIN_CACHE_SLOTS), :],
                              new_tile(wnew_ref, g, LANES)], axis=0)
        wv = jnp.concatenate([win_ref[pl.ds(kh + g, win_buf, stride=WIN_CACHE_SLOTS), :],
                              new_tile(wnew_ref, kh + g, LANES)], axis=0)
        wlane = lax.broadcasted_iota(jnp.int32, (ROWS_PAD, win_buf + LANES), 1)
        wpos = past - win_buf + wlane
        diff = qpos - wpos
        wmask = jnp.logical_and(jnp.logical_and(diff >= 0, diff <= WINDOW),
                                jnp.logical_and(wpos >= 0, wpos < past + n_new))
        o_win = attend(q, wk, wv, wmask)
        gates = gate_ref[g]
        o_ref[g] = gates[:, 0:1] * ocmp_ref[g] + gates[:, 1:2] * o_sel + gates[:, 2:3] * o_win


def _nsa_main_sample(idx, page_table, q_rot, cache, rows_new, win_state, win_new, o_cmp, gates, *, n_top):
    b, n_pages = page_table.shape
    past = n_pages * PAGE_SIZE
    per_page = PAGE_SIZE // SEL_BLOCK
    kh = NSA_KV_HEADS

    def sel_spec(g, n):
        def index(bi, idx_ref, pt):
            blk = jnp.maximum(idx_ref[(bi * kh + g) * n_top + n], 0)
            page = jnp.minimum(blk // per_page, n_pages - 1)
            return pt[bi, page], blk % per_page, 0
        return pl.BlockSpec((None, SEL_BLOCK * NSA_CACHE_SLOTS, HEAD_DIM), index)

    def per_seq(rows):
        return pl.BlockSpec((None, rows, HEAD_DIM), lambda bi, idx_ref, pt: (bi, 0, 0))

    head_spec = pl.BlockSpec((None, kh, ROWS_PAD, HEAD_DIM), lambda bi, idx_ref, pt: (bi, 0, 0, 0))
    grid_spec = pltpu.PrefetchScalarGridSpec(
        num_scalar_prefetch=2,
        grid=(b,),
        in_specs=[head_spec] + [sel_spec(g, n) for g in range(kh) for n in range(n_top)] + [
            per_seq(NSA_CACHE_SLOTS), per_seq(win_state.shape[1]), per_seq(WIN_CACHE_SLOTS), head_spec, head_spec],
        out_specs=head_spec,
    )
    return pl.pallas_call(
        functools.partial(_nsa_main_sample_kernel, n_top=n_top, past=past, n_new=1),
        grid_spec=grid_spec,
        out_shape=jax.ShapeDtypeStruct((b, kh, ROWS_PAD, HEAD_DIM), F32),
        compiler_params=_cparams(("parallel",)),
        name="nsa_main_sample",
    )(idx, page_table, q_rot, *([cache] * (kh * n_top)), rows_new, win_state, win_new, o_cmp, gates)


def _merge_kernel(osb_ref, onsa_ref, g0_ref, g1_ref, pa_ref, pb_ref, wo_ref, x_ref, out_ref, acc_ref):
    j = pl.program_id(1)

    @pl.when(j == 0)
    def _():
        acc_ref[...] = jnp.zeros_like(acc_ref)

    a = jnp.dot(osb_ref[...], pa_ref[...], preferred_element_type=F32)
    b = jnp.dot(onsa_ref[...], pb_ref[...], preferred_element_type=F32)
    m = (g0_ref[...] * a + g1_ref[...] * b).astype(BF16)
    acc_ref[...] += jnp.dot(m, wo_ref[...], preferred_element_type=F32)

    @pl.when(j == pl.num_programs(1) - 1)
    def _():
        out_ref[...] = x_ref[...] + acc_ref[...]


def _merge(o_sb, o_nsa, mg, pa, pb, wo, x, *, tm_pref=512, tn=1024):
    m, d = x.shape
    ka, kb = o_sb.shape[1], o_nsa.shape[1]
    tm = _pick_tile(m, tm_pref)
    tn = _pick_tile(d, tn)
    nj = d // tn
    return pl.pallas_call(
        _merge_kernel,
        grid=(m // tm, nj),
        in_specs=[
            pl.BlockSpec((tm, ka), lambda i, j: (i, 0)),
            pl.BlockSpec((tm, kb), lambda i, j: (i, 0)),
            pl.BlockSpec((tm, tn), lambda i, j: (i, j)),
            pl.BlockSpec((tm, tn), lambda i, j: (i, nj + j)),
            pl.BlockSpec((ka, tn), lambda i, j: (0, j)),
            pl.BlockSpec((kb, tn), lambda i, j: (0, j)),
            pl.BlockSpec((tn, d), lambda i, j: (j, 0)),
            pl.BlockSpec((tm, d), lambda i, j: (i, 0)),
        ],
        out_specs=pl.BlockSpec((tm, d), lambda i, j: (i, 0)),
        out_shape=jax.ShapeDtypeStruct((m, d), F32),
        scratch_shapes=[pltpu.VMEM((tm, d), F32)],
        compiler_params=_cparams(("parallel", "arbitrary")),
        name="merge",
    )(o_sb, o_nsa, mg, mg, pa, pb, wo, x)


SB_Q_COLS = SB_HEADS * HEAD_DIM
SB_KV_COLS = 2 * SB_HEADS * HEAD_DIM
NSA_Q_COLS = NSA_Q_HEADS * HEAD_DIM
NSA_ROW_COLS = 4 * NSA_KV_HEADS * HEAD_DIM
NSA_WIN_COLS = 2 * NSA_KV_HEADS * HEAD_DIM
NSA_GATE_COLS = 3 * NSA_Q_HEADS


def _prep_layer_weights(layer, g_ffn1, ffn1_w_gu, ffn1_w_down, g_mix, w_in, cmp_pe_k, cmp_w_k, cmp_pe_v, cmp_w_v,
                        p_a, p_b, w_o, g_ffn2, ffn2_w_gu, ffn2_w_down):
    w = {}
    w["g1"], w["g_mix"], w["g2"] = g_ffn1[layer][None], g_mix[layer][None], g_ffn2[layer][None]
    w["ffn1"] = _prep_ffn_weights(ffn1_w_gu[layer], ffn1_w_down[layer])
    w["ffn2"] = _prep_ffn_weights(ffn2_w_gu[layer], ffn2_w_down[layer])
    wi = w_in[layer]
    off = 0
    for name, n in (("sbq", SB_Q_COLS), ("sbkv", SB_KV_COLS), ("nq", NSA_Q_COLS), ("rows", NSA_ROW_COLS),
                    ("win", NSA_WIN_COLS)):
        w[name] = wi[:, off:off + n].astype(BF16)
        off += n
    wg = wi[:, off:off + NSA_GATE_COLS].reshape(-1, 3, NSA_KV_HEADS, NSA_GROUP).transpose(0, 2, 1, 3)
    wg = wg.reshape(-1, NSA_KV_HEADS, 3 * NSA_GROUP)
    wg = jnp.pad(wg, ((0, 0), (0, 0), (0, LANES - 3 * NSA_GROUP)))
    w["gate"] = wg.reshape(-1, NSA_KV_HEADS * LANES).astype(BF16)
    off += NSA_GATE_COLS
    w["mg"] = wi[:, off:].astype(BF16)
    w["pe"] = jnp.stack([cmp_pe_k[layer], cmp_pe_v[layer]])
    w["wc"] = jnp.stack([cmp_w_k[layer], cmp_w_v[layer]]).astype(BF16)
    w["pa"], w["pb"], w["wo"] = p_a[layer].astype(BF16), p_b[layer].astype(BF16), w_o[layer].astype(BF16)
    return w


def _mixer_inputs(h, w, cos2, sin2):
    plain_bf = ("plain", None, BF16)
    wide = 4 * LANES
    (sbq,) = _proj(h, w["sbq"], (plain_bf,), tn=wide)
    kv_f, kv_b = _proj(h, w["sbkv"], (("plain", None, F32), plain_bf), tn=wide)
    nq_raw, nq_rot = _proj(h, w["nq"], (plain_bf, ("rope", None, BF16)), cos2, sin2, tn=wide)
    rows_f, rows_b = _proj(h, w["rows"], (("rope", (2,), F32), ("rope", (2,), BF16)), cos2, sin2)
    win_f, win_b = _proj(h, w["win"], (("rope", (0,), F32), ("rope", (0,), BF16)), cos2, sin2)
    (gates,) = _proj(h, w["gate"], (("sigmoid", None, F32),), tn=LANES)
    (mg,) = _proj(h, w["mg"], (("sigmoid", None, F32),), tn=wide)
    return dict(sbq=sbq, kv_f=kv_f, kv_b=kv_b, nq_raw=nq_raw, nq_rot=nq_rot, rows_f=rows_f, rows_b=rows_b,
                win_f=win_f, win_b=win_b, gates=gates, mg=mg)


def _prompt_layer(x, w, cos2, sin2, g_next, last):
    t = x.shape[0]
    assert t % LANES == 0 and t >= CMP_BLOCK
    x1, h = _ffn(x, w["g1"], *w["ffn1"], w["g_mix"], emit_x=True, norm_dtype=BF16)
    mi = _mixer_inputs(h, w, cos2, sin2)
    o_sb = _sb_prompt(mi["sbq"], mi["kv_b"])
    p0, p1 = _compress_prompt(mi["rows_f"], w["pe"], w["wc"])
    n_cmp = (t - CMP_BLOCK) // CMP_STRIDE + 1
    n_sel = -(-t // SEL_BLOCK)
    tq = _pick_tile(t, 128)
    o_cmp, sel_t, _ = _nsa_cmp(mi["nq_raw"][None], p0, p1, tq=tq, qpos0=0, n_cmp=n_cmp, n_sel=n_sel)
    sv_cols = mi["rows_b"][:, 3 * NSA_KV_HEADS * HEAD_DIM:]
    sv_t = sv_cols.reshape(t, NSA_KV_HEADS, HEAD_DIM).transpose(1, 2, 0)
    wv_cols = mi["win_b"][:, NSA_KV_HEADS * HEAD_DIM:]
    wv_t = wv_cols.reshape(t, NSA_KV_HEADS, HEAD_DIM).transpose(1, 2, 0)
    o_nsa = _nsa_main_prompt(mi["nq_rot"], mi["win_b"], wv_t, mi["rows_b"], sv_t, sel_t[0], o_cmp[0], mi["gates"])
    x2 = _merge(o_sb, o_nsa, mi["mg"], w["pa"], w["pb"], w["wo"], x1)
    if last:
        (y,) = _ffn(x2, w["g2"], *w["ffn2"], g_next, emit_x=False, norm_dtype=F32)
        x3 = None
    else:
        x3, y = _ffn(x2, w["g2"], *w["ffn2"], g_next, emit_x=True, norm_dtype=F32)
    return x3, y, mi["kv_f"], mi["rows_f"], mi["win_f"]


def _pad_rows(a, rows):
    return jnp.pad(a, ((0, 0), (0, 0), (0, rows - a.shape[2]), (0, 0)))


def _sample_layer(x, w, cos2, sin2, cache_sb, cache_nsa, win_state, page_table, g_next, last):
    b = x.shape[0]
    n_pages = page_table.shape[1]
    past = n_pages * PAGE_SIZE
    total = past + 1
    kh, grp = NSA_KV_HEADS, NSA_GROUP
    x1, h = _ffn(x, w["g1"], *w["ffn1"], w["g_mix"], emit_x=True, norm_dtype=BF16)
    mi = _mixer_inputs(h, w, cos2, sin2)
    o_sb = _sb_sample(mi["sbq"].astype(F32).reshape(b, SB_HEADS, HEAD_DIM),
                      mi["kv_f"].reshape(b, SB_CACHE_SLOTS, HEAD_DIM), cache_sb, page_table)
    o_sb = o_sb.reshape(b, SB_HEADS * HEAD_DIM)

    p0, p1 = _compress_sample(cache_nsa, page_table, w["pe"], w["wc"])
    n_cmp = (total - CMP_BLOCK) // CMP_STRIDE + 1
    n_sel = -(-total // SEL_BLOCK)
    n_top = min(SEL_TOPN, n_sel)
    q_raw = jnp.pad(mi["nq_raw"][:, None, :], ((0, 0), (0, LANES - 1), (0, 0)))
    o_cmp, _, idx = _nsa_cmp(q_raw, p0, p1, tq=LANES, qpos0=past, n_cmp=n_cmp, n_sel=n_sel)
    idx = idx[:, :, :n_top, 0].reshape(-1)
    q_rot = _pad_rows(mi["nq_rot"].reshape(b, kh, grp, HEAD_DIM), ROWS_PAD)
    o_cmp = _pad_rows(o_cmp[:, 0, :].reshape(b, kh, grp, HEAD_DIM), ROWS_PAD)
    gates = mi["gates"].reshape(b, kh, LANES)[:, :, :3 * grp].reshape(b, kh, 3, grp).transpose(0, 1, 3, 2)
    gates = jnp.pad(gates, ((0, 0), (0, 0), (0, ROWS_PAD - grp), (0, LANES - 3)))
    o_nsa = _nsa_main_sample(idx, page_table, q_rot, cache_nsa, mi["rows_f"].reshape(b, NSA_CACHE_SLOTS, HEAD_DIM),
                             win_state, mi["win_f"].reshape(b, WIN_CACHE_SLOTS, HEAD_DIM), o_cmp, gates, n_top=n_top)
    o_nsa = o_nsa[:, :, :grp, :].reshape(b, NSA_Q_HEADS * HEAD_DIM)

    x2 = _merge(o_sb.astype(BF16), o_nsa.astype(BF16), mi["mg"], w["pa"], w["pb"], w["wo"], x1)
    if last:
        (y,) = _ffn(x2, w["g2"], *w["ffn2"], g_next, emit_x=False, norm_dtype=F32)
        x3 = None
    else:
        x3, y = _ffn(x2, w["g2"], *w["ffn2"], g_next, emit_x=True, norm_dtype=F32)
    return x3, y, mi["kv_f"], mi["rows_f"], mi["win_f"]


def kernel(x_prompt, x_sample, cache_sb_kv, cache_nsa_kv, state_win_kv, page_table, g_ffn1, ffn1_w_gu, ffn1_w_down,
           g_mix, w_in, cmp_pe_k, cmp_w_k, cmp_pe_v, cmp_w_v, p_a, p_b, w_o, g_ffn2, ffn2_w_gu, ffn2_w_down, g_final):
    bp, t, d = x_prompt.shape
    db, ds, _ = x_sample.shape
    depth = g_ffn1.shape[0]
    n_pool = cache_sb_kv.shape[1]
    assert bp == 1 and ds == 1
    past = page_table.shape[1] * PAGE_SIZE
    win_buf = state_win_kv.shape[2]
    cos_p, sin_p = _rope_tables(jnp.arange(t))
    cos_s, sin_s = _rope_tables(jnp.full((db,), past, jnp.int32))
    g_fin = g_final[None]

    xp = x_prompt.reshape(t, d)
    xs = x_sample.reshape(db, d)
    sb_p, nsa_p, win_p, sb_s, nsa_s, win_s = [], [], [], [], [], []
    yp = ys = None
    for layer in range(depth):
        w = _prep_layer_weights(layer, g_ffn1, ffn1_w_gu, ffn1_w_down, g_mix, w_in, cmp_pe_k, cmp_w_k, cmp_pe_v,
                                cmp_w_v, p_a, p_b, w_o, g_ffn2, ffn2_w_gu, ffn2_w_down)
        last = layer == depth - 1
        xp, yp, kv_f, rows_f, win_f = _prompt_layer(xp, w, cos_p, sin_p, g_fin, last)
        sb_p.append(kv_f.reshape(1, t, 2, SB_HEADS, HEAD_DIM))
        nsa_p.append(rows_f.reshape(1, t, 4, NSA_KV_HEADS, HEAD_DIM))
        wp = min(WINDOW, t)
        win_p.append(win_f[t - wp:].reshape(1, wp, 2, NSA_KV_HEADS, HEAD_DIM))

        cache_sb = cache_sb_kv[layer].reshape(n_pool, PAGE_SIZE * SB_CACHE_SLOTS, HEAD_DIM)
        cache_nsa = cache_nsa_kv[layer].reshape(n_pool, PAGE_SIZE * NSA_CACHE_SLOTS, HEAD_DIM)
        win_state = state_win_kv[layer].reshape(db, win_buf * WIN_CACHE_SLOTS, HEAD_DIM)
        xs, ys, kv_f, rows_f, win_f = _sample_layer(xs, w, cos_s, sin_s, cache_sb, cache_nsa, win_state, page_table,
                                                    g_fin, last)
        sb_s.append(kv_f.reshape(db, 1, 2, SB_HEADS, HEAD_DIM))
        nsa_s.append(rows_f.reshape(db, 1, 4, NSA_KV_HEADS, HEAD_DIM))
        win_all = jnp.concatenate([state_win_kv[layer], win_f.reshape(db, 1, 2, NSA_KV_HEADS, HEAD_DIM)], axis=1)
        win_s.append(win_all[:, win_all.shape[1] - win_buf:])
    return (yp.reshape(1, t, d), ys.reshape(db, 1, d), jnp.stack(sb_p), jnp.stack(nsa_p), jnp.stack(win_p),
            jnp.stack(sb_s), jnp.stack(nsa_s), jnp.stack(win_s))
```
